```python
import math
import jax, jax.numpy as jnp
from jax import lax
import numpy as np

D_MODEL = 2048
BATCH = 8
SEQ = 4096
DEPTH = 2

N_HEADS_TOTAL = 16
HEAD_DIM = D_MODEL // N_HEADS_TOTAL
SWA_HEADS = 6
SWA_KV_HEADS = 2
SWA_GROUP = SWA_HEADS // SWA_KV_HEADS
WINDOW = 128
MOBA_HEADS = 4
MOBA_BLOCK = 256
MOBA_TOPK = 3
MOBA_Q_CHUNK = 64
DIFF_HEADS = 6
DIFF_QK_DIM = HEAD_DIM // 2
DIFF_V_DIM = HEAD_DIM
DIFF_Q_BLOCK = 128
MIX_WIDTH = (SWA_HEADS + MOBA_HEADS) * HEAD_DIM + DIFF_HEADS * DIFF_V_DIM
IN_SPLITS = (SWA_HEADS * HEAD_DIM, SWA_KV_HEADS * HEAD_DIM, SWA_KV_HEADS * HEAD_DIM,
             MOBA_HEADS * HEAD_DIM, MOBA_HEADS * HEAD_DIM, MOBA_HEADS * HEAD_DIM,
             DIFF_HEADS * 2 * DIFF_QK_DIM, DIFF_HEADS * 2 * DIFF_QK_DIM, DIFF_HEADS * DIFF_V_DIM)
IN_WIDTH = sum(IN_SPLITS)
IN_OFFSETS = tuple(int(o) for o in np.cumsum(IN_SPLITS)[:-1])
REL_BUCKETS = 32
REL_EXACT = REL_BUCKETS // 2
REL_MAX_DIST = 128
N_GROUPS = 4
EXPERTS_PER_GROUP = 8
N_EXPERTS = N_GROUPS * EXPERTS_PER_GROUP
TOP_E = 2
D_EXPERT = 512
MOE_BLOCK = 256
PLE_DIM = 256
EPS = 1e-6
NEG = -1e30

kernel_name = 'hymba_style_hybrid_swa_moba_diff_hmoe'


def rms_norm(x, g):
    xf = x.astype(jnp.float32)
    y = xf * lax.rsqrt(jnp.mean(xf * xf, axis=-1, keepdims=True) + EPS)
    return (y * g.astype(jnp.float32)).astype(x.dtype)


def rel_bucket(dist):
    n = jnp.maximum(dist, 0)
    is_small = n < REL_EXACT
    nf = jnp.maximum(n, 1).astype(jnp.float32)
    large = REL_EXACT + (jnp.log(nf / REL_EXACT) / math.log(REL_MAX_DIST / REL_EXACT)
                         * (REL_BUCKETS - REL_EXACT)).astype(jnp.int32)
    large = jnp.minimum(large, REL_BUCKETS - 1)
    return jnp.where(is_small, n, large)


def swa_sink_attention(q, k, v, bias_tab, sinks):
    b, s = q.shape[:2]
    nw = s // WINDOW
    qb = q.reshape(b, nw, WINDOW, SWA_KV_HEADS, SWA_GROUP, HEAD_DIM)
    kb = k.reshape(b, nw, WINDOW, SWA_KV_HEADS, HEAD_DIM)
    vb = v.reshape(b, nw, WINDOW, SWA_KV_HEADS, HEAD_DIM)
    pad = jnp.zeros_like(kb[:, :1])
    kcat = jnp.concatenate([jnp.concatenate([pad, kb[:, :-1]], axis=1), kb], axis=2)
    vcat = jnp.concatenate([jnp.concatenate([pad, vb[:, :-1]], axis=1), vb], axis=2)
    logits = jnp.einsum('bnqkgd,bnpkd->bkgnqp', qb, kcat).astype(jnp.float32) * (HEAD_DIM ** -0.5)
    qi = jnp.arange(WINDOW)[:, None]
    pj = jnp.arange(2 * WINDOW)[None, :]
    dist = qi + WINDOW - pj
    bias = bias_tab[rel_bucket(dist)].astype(jnp.float32)
    bias = bias.reshape(WINDOW, 2 * WINDOW, SWA_KV_HEADS, SWA_GROUP).transpose(2, 3, 0, 1)[None, :, :, None]
    in_window = (dist >= 0) & (dist < WINDOW)
    key_exists = (jnp.arange(nw)[:, None, None] > 0) | (pj >= WINDOW)[None]
    mask = in_window[None] & key_exists
    logits = jnp.where(mask, logits + bias, NEG)
    sink = sinks.astype(jnp.float32).reshape(SWA_KV_HEADS, SWA_GROUP)[None, :, :, None, None, None]
    sink = jnp.broadcast_to(sink, logits.shape[:-1] + (1,))
    probs = jax.nn.softmax(jnp.concatenate([logits, sink], axis=-1), axis=-1)[..., :-1]
    out = jnp.einsum('bkgnqp,bnpkd->bnqkgd', probs.astype(v.dtype), vcat)
    return out.reshape(b, s, SWA_HEADS * HEAD_DIM)


def moba_attention(q, k, v, bias_tab):
    b, s = q.shape[:2]
    nb = -(-s // MOBA_BLOCK)
    s_pad = nb * MOBA_BLOCK
    padw = ((0, 0), (0, s_pad - s), (0, 0), (0, 0))
    kb = jnp.pad(k, padw).reshape(b, nb, MOBA_BLOCK, MOBA_HEADS, HEAD_DIM).transpose(0, 3, 1, 2, 4)
    vb = jnp.pad(v, padw).reshape(b, nb, MOBA_BLOCK, MOBA_HEADS, HEAD_DIM).transpose(0, 3, 1, 2, 4)
    kmean = jnp.mean(kb.astype(jnp.float32), axis=3)
    topk = min(MOBA_TOPK, nb)
    nc = s // MOBA_Q_CHUNK
    qc = q.reshape(b, nc, MOBA_Q_CHUNK, MOBA_HEADS, HEAD_DIM).transpose(1, 0, 2, 3, 4)
    bi = jnp.arange(b)[:, None, None, None]
    hi = jnp.arange(MOBA_HEADS)[None, None, :, None]
    offs = jnp.arange(MOBA_BLOCK)
    scale = HEAD_DIM ** -0.5

    def chunk(args):
        qblk, c = args
        qpos = c * MOBA_Q_CHUNK + jnp.arange(MOBA_Q_CHUNK)
        j = (c * MOBA_Q_CHUNK) // MOBA_BLOCK
        gate = jnp.einsum('bqhd,bhnd->bqhn', qblk.astype(jnp.float32), kmean)
        gate = jnp.where(jnp.arange(nb) < j, gate, -jnp.inf)
        _, idx = lax.top_k(gate, topk)
        valid = jnp.arange(topk) < j
        kg = kb[bi, hi, idx]
        vg = vb[bi, hi, idx]
        l_sel = jnp.einsum('bqhd,bqhtsd->bqhts', qblk, kg).astype(jnp.float32) * scale
        kpos_sel = idx[..., None] * MOBA_BLOCK + offs
        dist_sel = qpos[None, :, None, None, None] - kpos_sel
        bias_sel = bias_tab[rel_bucket(dist_sel), hi[..., None]].astype(jnp.float32)
        l_sel = jnp.where(valid[:, None], l_sel + bias_sel, NEG)
        kown = lax.dynamic_index_in_dim(kb, j, axis=2, keepdims=False)
        vown = lax.dynamic_index_in_dim(vb, j, axis=2, keepdims=False)
        l_own = jnp.einsum('bqhd,bhsd->bqhs', qblk, kown).astype(jnp.float32) * scale
        dist_own = qpos[:, None] - (j * MOBA_BLOCK + offs)[None, :]
        bias_own = bias_tab[rel_bucket(dist_own)].astype(jnp.float32).transpose(0, 2, 1)[None]
        l_own = jnp.where((dist_own >= 0)[None, :, None, :], l_own + bias_own, NEG)
        logits = jnp.concatenate([l_sel.reshape(b, MOBA_Q_CHUNK, MOBA_HEADS, topk * MOBA_BLOCK), l_own], axis=-1)
        probs = jax.nn.softmax(logits, axis=-1).astype(v.dtype)
        p_sel = probs[..., :topk * MOBA_BLOCK].reshape(b, MOBA_Q_CHUNK, MOBA_HEADS, topk, MOBA_BLOCK)
        p_own = probs[..., topk * MOBA_BLOCK:]
        return (jnp.einsum('bqhts,bqhtsd->bqhd', p_sel, vg)
                + jnp.einsum('bqhs,bhsd->bqhd', p_own, vown))

    out = lax.map(chunk, (qc, jnp.arange(nc)))
    return out.transpose(1, 0, 2, 3, 4).reshape(b, s, MOBA_HEADS * HEAD_DIM)


def diff_attention(q, k, v, bias_tab, lam, sub_g, lambda_init):
    b, s = q.shape[:2]
    nq = s // DIFF_Q_BLOCK
    qblocks = q.reshape(b, nq, DIFF_Q_BLOCK, DIFF_HEADS, 2, DIFF_QK_DIM).transpose(1, 0, 2, 3, 4, 5)
    kpos = jnp.arange(s)
    scale = DIFF_QK_DIM ** -0.5

    def block(args):
        qblk, c = args
        qpos = c * DIFF_Q_BLOCK + jnp.arange(DIFF_Q_BLOCK)
        logits = jnp.einsum('bqhcd,bkhcd->bhcqk', qblk, k).astype(jnp.float32) * scale
        dist = qpos[:, None] - kpos[None, :]
        bias = bias_tab[rel_bucket(dist)].astype(jnp.float32).transpose(2, 0, 1)[None, :, None]
        logits = jnp.where(dist >= 0, logits + bias, NEG)
        probs = jax.nn.softmax(logits, axis=-1)
        attn = probs[:, :, 0] - lam * probs[:, :, 1]
        return jnp.einsum('bhqk,bkhe->bqhe', attn.astype(v.dtype), v)

    out = lax.map(block, (qblocks, jnp.arange(nq)))
    out = out.transpose(1, 0, 2, 3, 4).reshape(b, s, DIFF_HEADS, DIFF_V_DIM)
    out = rms_norm(out, sub_g) * (1.0 - lambda_init)
    return out.reshape(b, s, DIFF_HEADS * DIFF_V_DIM)


def hier_moe(v, w_group, w_expert, w1, w3, w2):
    b, s, d = v.shape
    t = b * s
    xf = v.reshape(t, d)
    g_prob = jax.nn.softmax((xf @ w_group).astype(jnp.float32), axis=-1)
    grp = jnp.argmax(g_prob, axis=-1)
    g_w = jnp.take_along_axis(g_prob, grp[:, None], axis=1)
    e_logits = (xf @ w_expert).astype(jnp.float32).reshape(t, N_GROUPS, EXPERTS_PER_GROUP)
    e_in = jnp.take_along_axis(e_logits, grp[:, None, None], axis=1)[:, 0]
    top_l, top_i = lax.top_k(e_in, TOP_E)
    gate = g_w * jax.nn.softmax(top_l, axis=-1)
    n_assign = t * TOP_E
    eid = (grp[:, None] * EXPERTS_PER_GROUP + top_i).reshape(n_assign)
    tok = jnp.arange(n_assign) // TOP_E
    wts = gate.reshape(n_assign)
    order = jnp.argsort(eid)
    se = eid[order]
    counts = jnp.bincount(eid, length=N_EXPERTS)
    starts = jnp.cumsum(counts) - counts
    pcounts = ((counts + MOE_BLOCK - 1) // MOE_BLOCK) * MOE_BLOCK
    pends = jnp.cumsum(pcounts)
    pstarts = pends - pcounts
    dest = pstarts[se] + (jnp.arange(n_assign) - starts[se])
    p_rows = n_assign + N_EXPERTS * MOE_BLOCK
    buf_tok = jnp.zeros((p_rows,), jnp.int32).at[dest].set(tok[order].astype(jnp.int32))
    buf_w = jnp.zeros((p_rows,), jnp.float32).at[dest].set(wts[order])
    nblk = p_rows // MOE_BLOCK
    blk_e = jnp.minimum(jnp.searchsorted(pends, jnp.arange(nblk) * MOE_BLOCK, side='right'), N_EXPERTS - 1)

    def expert_block(args):
        toks, e = args
        xb = xf[toks]
        hdn = jax.nn.silu(xb @ w1[e]) * (xb @ w3[e])
        return hdn @ w2[e]

    yb = lax.map(expert_block, (buf_tok.reshape(nblk, MOE_BLOCK), blk_e))
    y = jax.ops.segment_sum(yb.reshape(p_rows, d) * buf_w[:, None].astype(yb.dtype), buf_tok, num_segments=t)
    return y.reshape(b, s, d)


def setup_inputs(seed: int = 0) -> dict:
    key = jax.random.key(seed)
    ks = jax.random.split(key, 24)

    def nrm(k, shape, scale):
        return jax.random.normal(k, shape, jnp.float32) * scale

    return {
        'x': nrm(ks[0], (BATCH, SEQ, D_MODEL), 1.0),
        'p': nrm(ks[1], (DEPTH, BATCH, SEQ, PLE_DIM), 1.0),
        'rel_bias': nrm(ks[2], (REL_BUCKETS, N_HEADS_TOTAL), 0.5),
        'g_final': 1.0 + nrm(ks[3], (D_MODEL,), 0.05),
        'g_mix': 1.0 + nrm(ks[4], (DEPTH, D_MODEL), 0.05),
        'w_in': nrm(ks[5], (DEPTH, D_MODEL, IN_WIDTH), D_MODEL ** -0.5),
        'swa_sinks': nrm(ks[6], (DEPTH, SWA_HEADS), 0.5),
        'lam_q1': nrm(ks[7], (DEPTH, DIFF_QK_DIM), 0.1),
        'lam_k1': nrm(ks[8], (DEPTH, DIFF_QK_DIM), 0.1),
        'lam_q2': nrm(ks[9], (DEPTH, DIFF_QK_DIM), 0.1),
        'lam_k2': nrm(ks[10], (DEPTH, DIFF_QK_DIM), 0.1),
        'diff_subln': 1.0 + nrm(ks[11], (DEPTH, DIFF_V_DIM), 0.05),
        'w_out': nrm(ks[12], (DEPTH, MIX_WIDTH, D_MODEL), MIX_WIDTH ** -0.5),
        'g_ffn': 1.0 + nrm(ks[13], (DEPTH, D_MODEL), 0.05),
        'w_group': nrm(ks[14], (DEPTH, D_MODEL, N_GROUPS), D_MODEL ** -0.5),
        'w_expert': nrm(ks[15], (DEPTH, D_MODEL, N_EXPERTS), D_MODEL ** -0.5),
        'w1': nrm(ks[16], (DEPTH, N_EXPERTS, D_MODEL, D_EXPERT), D_MODEL ** -0.5),
        'w3': nrm(ks[17], (DEPTH, N_EXPERTS, D_MODEL, D_EXPERT), D_MODEL ** -0.5),
        'w2': nrm(ks[18], (DEPTH, N_EXPERTS, D_EXPERT, D_MODEL), D_EXPERT ** -0.5),
        'g_ple': 1.0 + nrm(ks[19], (DEPTH, D_MODEL), 0.05),
        'w_ple_gate': nrm(ks[20], (DEPTH, D_MODEL, D_MODEL), D_MODEL ** -0.5),
        'w_ple_proj': nrm(ks[21], (DEPTH, PLE_DIM, D_MODEL), PLE_DIM ** -0.5),
    }


def reference(x, p, rel_bias, g_final, g_mix, w_in, swa_sinks, lam_q1, lam_k1, lam_q2, lam_k2,
              diff_subln, w_out, g_ffn, w_group, w_expert, w1, w3, w2, g_ple, w_ple_gate, w_ple_proj):
    b, s, _ = x.shape
    bias_a = rel_bias[:, :SWA_HEADS]
    bias_b = rel_bias[:, SWA_HEADS:SWA_HEADS + MOBA_HEADS]
    bias_c = rel_bias[:, SWA_HEADS + MOBA_HEADS:]
    h = x
    for i in range(DEPTH):
        u = rms_norm(h, g_mix[i])
        proj = u @ w_in[i]
        qa, ka, va, qb, kb, vb, qc, kc, vc = jnp.split(proj, IN_OFFSETS, axis=-1)
        ya = swa_sink_attention(qa.reshape(b, s, SWA_HEADS, HEAD_DIM),
                                ka.reshape(b, s, SWA_KV_HEADS, HEAD_DIM),
                                va.reshape(b, s, SWA_KV_HEADS, HEAD_DIM), bias_a, swa_sinks[i])
        yb = moba_attention(qb.reshape(b, s, MOBA_HEADS, HEAD_DIM),
                            kb.reshape(b, s, MOBA_HEADS, HEAD_DIM),
                            vb.reshape(b, s, MOBA_HEADS, HEAD_DIM), bias_b)
        lambda_init = 0.8 - 0.6 * math.exp(-0.3 * i)
        lam = (jnp.exp(jnp.sum(lam_q1[i].astype(jnp.float32) * lam_k1[i].astype(jnp.float32)))
               - jnp.exp(jnp.sum(lam_q2[i].astype(jnp.float32) * lam_k2[i].astype(jnp.float32)))
               + lambda_init)
        yc = diff_attention(qc.reshape(b, s, DIFF_HEADS, 2, DIFF_QK_DIM),
                            kc.reshape(b, s, DIFF_HEADS, 2, DIFF_QK_DIM),
                            vc.reshape(b, s, DIFF_HEADS, DIFF_V_DIM), bias_c, lam, diff_subln[i], lambda_init)
        h = h + jnp.concatenate([ya, yb, yc], axis=-1) @ w_out[i]
        h = h + hier_moe(rms_norm(h, g_ffn[i]), w_group[i], w_expert[i], w1[i], w3[i], w2[i])
        h = h + jax.nn.sigmoid(rms_norm(h, g_ple[i]) @ w_ple_gate[i]) * (p[i] @ w_ple_proj[i])
    return rms_norm(h, g_final)
```

```python
import functools
import math

import jax
import jax.numpy as jnp
import numpy as np
from jax import lax
from jax.experimental import pallas as pl
from jax.experimental.pallas import tpu as pltpu

D_MODEL = 2048
HEAD_DIM = 128
SWA_HEADS = 6
SWA_KV_HEADS = 2
SWA_GROUP = SWA_HEADS // SWA_KV_HEADS
WINDOW = 128
MOBA_HEADS = 4
MOBA_BLOCK = 256
MOBA_TOPK = 3
DIFF_HEADS = 6
DIFF_QK_DIM = HEAD_DIM // 2
IN_WIDTH = 5120
REL_BUCKETS = 32
REL_EXACT = REL_BUCKETS // 2
REL_MAX_DIST = 128
N_GROUPS = 4
EXPERTS_PER_GROUP = 8
N_EXPERTS = N_GROUPS * EXPERTS_PER_GROUP
D_EXPERT = 512
MOE_BLOCK = 256
PLE_DIM = 256
EPS = 1e-6
NEG = -1e30

_QA, _KA, _VA = 0, 6, 8
_QB, _KB, _VB = 10, 14, 18
_QC, _KC, _VC = 22, 28, 34

ATT_TILE = 256
ROUTE_LANES = 128
EXPERT_LANE0 = N_GROUPS
VMEM_LIMIT_BYTES = 56 * 1024 * 1024

F32 = jnp.float32
BF16 = jnp.bfloat16


def _params(semantics):
    return pltpu.CompilerParams(dimension_semantics=semantics, vmem_limit_bytes=VMEM_LIMIT_BYTES)


def _rms(x, g):
    ms = jnp.mean(x * x, axis=-1, keepdims=True)
    return x * lax.rsqrt(ms + EPS) * g


def _dot(a, b):
    return jnp.dot(a, b, preferred_element_type=F32)


def _dot_nt(a, b):
    return lax.dot_general(a, b, (((1,), (1,)), ((), ())), preferred_element_type=F32)


def _split_bf16(x):
    hi = x.astype(BF16)
    lo = (x - hi.astype(F32)).astype(BF16)
    return hi, lo


def _bucket_table(max_dist):
    n = np.arange(max_dist)
    nf = np.maximum(n, 1).astype(np.float32)
    large = REL_EXACT + (np.log(nf / np.float32(REL_EXACT)) / np.float32(math.log(REL_MAX_DIST / REL_EXACT))
                         * np.float32(REL_BUCKETS - REL_EXACT)).astype(np.int32)
    large = np.minimum(large, REL_BUCKETS - 1)
    return np.where(n < REL_EXACT, n, large).astype(np.int32)


def _swa_bias(bias_a):
    qi = np.arange(WINDOW)[:, None]
    pj = np.arange(2 * WINDOW)[None, :]
    dist = qi + WINDOW - pj
    inside = (dist >= 0) & (dist < WINDOW)
    bucket = _bucket_table(2 * WINDOW)[np.maximum(dist, 0)]
    vals = jnp.transpose(bias_a[bucket], (2, 0, 1))
    return jnp.where(inside[None], vals, NEG).astype(F32)


def _tile_bias(bias_h):
    t = ATT_TILE
    i = np.arange(t)[:, None]
    j = np.arange(t)[None, :]
    table = _bucket_table(2 * t)
    d0 = i - j
    diag = jnp.where((d0 >= 0)[None], jnp.transpose(bias_h[table[np.maximum(d0, 0)]], (2, 0, 1)), NEG)
    prev = jnp.transpose(bias_h[table[t + d0]], (2, 0, 1))
    assert int(table[t + 1]) == REL_BUCKETS - 1
    far = jnp.broadcast_to(bias_h[REL_BUCKETS - 1][:, None, None], prev.shape)
    return jnp.stack([diag, prev, far], axis=1).astype(F32)


def _inproj_kernel(x_ref, g_ref, w_ref, o_ref, xn_ref):
    @pl.when(pl.program_id(1) == 0)
    def _():
        xn_ref[...] = _rms(x_ref[...], g_ref[...]).astype(BF16)

    o_ref[...] = _dot(xn_ref[...], w_ref[...]).astype(o_ref.dtype)


def _inproj(h, g, w, *, tm=512, tn=1280):
    t, d = h.shape
    n = w.shape[1]
    return pl.pallas_call(
        _inproj_kernel,
        grid=(t // tm, n // tn),
        in_specs=[pl.BlockSpec((tm, d), lambda i, j: (i, 0)),
                  pl.BlockSpec((1, d), lambda i, j: (0, 0)),
                  pl.BlockSpec((d, tn), lambda i, j: (0, j))],
        out_specs=pl.BlockSpec((tm, tn), lambda i, j: (i, j)),
        out_shape=jax.ShapeDtypeStruct((t, n), BF16),
        scratch_shapes=[pltpu.VMEM((tm, d), BF16)],
        compiler_params=_params(("parallel", "arbitrary")),
        name="inproj",
    )(h, g, w)


def _outproj_kernel(ya_ref, yb_ref, yc_ref, w_ref, h_ref, o_ref):
    na, nb = ya_ref.shape[1], yb_ref.shape[1]
    y = _dot(ya_ref[...], w_ref[0:na, :])
    y += _dot(yb_ref[...], w_ref[na:na + nb, :])
    y += _dot(yc_ref[...], w_ref[na + nb:, :])
    o_ref[...] = h_ref[...] + y


def _outproj(ya, yb, yc, w, h, *, tm=256):
    t, d = h.shape
    row = lambda width: pl.BlockSpec((tm, width), lambda i: (i, 0))
    return pl.pallas_call(
        _outproj_kernel,
        grid=(t // tm,),
        in_specs=[row(ya.shape[1]), row(yb.shape[1]), row(yc.shape[1]),
                  pl.BlockSpec(w.shape, lambda i: (0, 0)), row(d)],
        out_specs=row(d),
        out_shape=jax.ShapeDtypeStruct((t, d), F32),
        compiler_params=_params(("parallel",)),
        name="outproj",
    )(ya, yb, yc, w, h)


def _ple_kernel(h_ref, g_ref, wg_ref, p_ref, wp_ref, gf_ref, o_ref, *, final):
    h = h_ref[...]
    u = _rms(h, g_ref[...]).astype(BF16)
    gate = jax.nn.sigmoid(_dot(u, wg_ref[...]))
    o = h + gate * _dot(p_ref[...].astype(BF16), wp_ref[...])
    if final:
        o = _rms(o, gf_ref[...])
    o_ref[...] = o


def _ple(h, g, wg, p, wp, g_final, *, final, tm=256):
    t, d = h.shape
    row = lambda width: pl.BlockSpec((tm, width), lambda i: (i, 0))
    full = lambda a: pl.BlockSpec(a.shape, lambda i: (0, 0))
    return pl.pallas_call(
        functools.partial(_ple_kernel, final=final),
        grid=(t // tm,),
        in_specs=[row(d), full(g), full(wg), row(p.shape[1]), full(wp), full(g_final)],
        out_specs=row(d),
        out_shape=jax.ShapeDtypeStruct((t, d), F32),
        compiler_params=_params(("parallel",)),
        name="ple",
    )(h, g, wg, p, wp, g_final)


def _swa_kernel(sink_ref, q_ref, kp_ref, kc_ref, vp_ref, vc_ref, bias_ref, o_ref):
    n = pl.program_id(1)
    w = WINDOW
    scale = HEAD_DIM ** -0.5
    col = lax.broadcasted_iota(jnp.int32, (w, 2 * w), 1)
    prev_pen = jnp.where(col < w, jnp.where(n == 0, NEG, 0.0).astype(F32), 0.0)
    for kv in range(SWA_KV_HEADS):
        cols = slice(kv * HEAD_DIM, (kv + 1) * HEAD_DIM)
        kcat = jnp.concatenate([kp_ref[:, cols], kc_ref[:, cols]], axis=0)
        vcat = jnp.concatenate([vp_ref[:, cols], vc_ref[:, cols]], axis=0)
        for g in range(SWA_GROUP):
            hd = kv * SWA_GROUP + g
            hcols = slice(hd * HEAD_DIM, (hd + 1) * HEAD_DIM)
            s = _dot_nt(q_ref[:, hcols], kcat) * scale + bias_ref[hd] + prev_pen
            sink = sink_ref[hd]
            m = jnp.maximum(jnp.max(s, axis=1, keepdims=True), sink)
            p = jnp.exp(s - m)
            denom = jnp.sum(p, axis=1, keepdims=True) + jnp.exp(sink - m)
            o = _dot(p.astype(BF16), vcat) / denom
            o_ref[:, hcols] = o.astype(o_ref.dtype)


def _swa(proj, bias, sinks):
    b, s, _ = proj.shape
    w = WINDOW
    qw, kw = SWA_HEADS * HEAD_DIM, SWA_KV_HEADS * HEAD_DIM
    prev = lambda blk: pl.BlockSpec((None, w, kw), lambda bi, n: (bi, jnp.maximum(n - 1, 0), blk))
    cur = lambda blk: pl.BlockSpec((None, w, kw), lambda bi, n: (bi, n, blk))
    kblk, vblk = (_KA * HEAD_DIM) // kw, (_VA * HEAD_DIM) // kw
    return pl.pallas_call(
        _swa_kernel,
        grid=(b, s // w),
        in_specs=[pl.BlockSpec(memory_space=pltpu.SMEM),
                  pl.BlockSpec((None, w, qw), lambda bi, n: (bi, n, 0)),
                  prev(kblk), cur(kblk), prev(vblk), cur(vblk),
                  pl.BlockSpec(bias.shape, lambda bi, n: (0, 0, 0))],
        out_specs=pl.BlockSpec((None, w, qw), lambda bi, n: (bi, n, 0)),
        out_shape=jax.ShapeDtypeStruct((b, s, qw), BF16),
        compiler_params=_params(("parallel", "parallel")),
        name="swa",
    )(sinks, proj, proj, proj, proj, proj, bias)


def _softmax_step(s, vt, carry):
    m, l, acc = carry
    m_new = jnp.maximum(m, jnp.max(s, axis=1, keepdims=True))
    alpha = jnp.exp(m - m_new)
    p = jnp.exp(s - m_new)
    l = alpha * l + jnp.sum(p, axis=1, keepdims=True)
    acc = alpha * acc + _dot(p.astype(BF16), vt)
    return m_new, l, acc


def _softmax_init(rows):
    return (jnp.full((rows, 1), NEG, F32), jnp.zeros((rows, 1), F32), jnp.zeros((rows, HEAD_DIM), F32))


def _moba_kernel(q_ref, k_ref, v_ref, bias_ref, o_ref, km_ref, pen_ref):
    j = pl.program_id(2)
    t = ATT_TILE
    nb = k_ref.shape[0] // t
    scale = HEAD_DIM ** -0.5

    @pl.when(j == 0)
    def _():
        km_ref[...] = jnp.zeros_like(km_ref)
        for n in range(nb):
            km_ref[n:n + 1, :] = jnp.mean(k_ref[n * t:(n + 1) * t, :].astype(F32), axis=0, keepdims=True)

    q = q_ref[...]
    km_hi, km_lo = _split_bf16(km_ref[...])
    gate = _dot_nt(q, km_hi) + _dot_nt(q, km_lo)
    lane = lax.broadcasted_iota(jnp.int32, gate.shape, 1)
    lane_f = lane.astype(F32)
    past = lane < j
    gate = jnp.where(past, gate, -jnp.inf)
    pen = jnp.full(gate.shape, NEG, F32)
    for _ in range(MOBA_TOPK):
        best = jnp.max(gate, axis=1, keepdims=True)
        first = jnp.min(jnp.where(gate == best, lane_f, float(ROUTE_LANES)), axis=1, keepdims=True)
        pick = (lane_f == first) & past
        pen = jnp.where(pick, 0.0, pen)
        gate = jnp.where(pick, -jnp.inf, gate)
    pen_ref[...] = jnp.where(lane == j, 0.0, pen)

    def body(n, carry):
        start = pl.multiple_of(n * t, t)
        kt = k_ref[pl.ds(start, t), :]
        vt = v_ref[pl.ds(start, t), :]
        s = _dot_nt(q, kt) * scale + bias_ref[jnp.minimum(j - n, 2)]
        s = s + jnp.sum(jnp.where(lane == n, pen_ref[...], 0.0), axis=1, keepdims=True)
        return _softmax_step(s, vt, carry)

    _, l, acc = lax.fori_loop(0, j + 1, body, _softmax_init(t))
    o_ref[...] = (acc / l).astype(o_ref.dtype)


def _moba(proj, bias):
    b, s, _ = proj.shape
    t = ATT_TILE
    return pl.pallas_call(
        _moba_kernel,
        grid=(b, MOBA_HEADS, s // t),
        in_specs=[pl.BlockSpec((None, t, HEAD_DIM), lambda bi, h, j: (bi, j, _QB + h)),
                  pl.BlockSpec((None, s, HEAD_DIM), lambda bi, h, j: (bi, 0, _KB + h)),
                  pl.BlockSpec((None, s, HEAD_DIM), lambda bi, h, j: (bi, 0, _VB + h)),
                  pl.BlockSpec((None, 3, t, t), lambda bi, h, j: (h, 0, 0, 0))],
        out_specs=pl.BlockSpec((None, t, HEAD_DIM), lambda bi, h, j: (bi, j, h)),
        out_shape=jax.ShapeDtypeStruct((b, s, MOBA_HEADS * HEAD_DIM), BF16),
        scratch_shapes=[pltpu.VMEM((ROUTE_LANES, HEAD_DIM), F32), pltpu.VMEM((t, ROUTE_LANES), F32)],
        compiler_params=_params(("arbitrary", "arbitrary", "arbitrary")),
        name="moba",
    )(proj, proj, proj, bias)


def _diff_kernel(q_ref, k_ref, v_ref, bias_ref, lam_ref, subg_ref, o_ref, *, lambda_init):
    qi = pl.program_id(2)
    t = ATT_TILE
    q = (q_ref[...].astype(F32) * (DIFF_QK_DIM ** -0.5)).astype(BF16)
    lane = lax.broadcasted_iota(jnp.int32, q.shape, 1)
    zero = jnp.zeros_like(q)
    q1 = jnp.where(lane < DIFF_QK_DIM, q, zero)
    q2 = jnp.where(lane >= DIFF_QK_DIM, q, zero)

    def body(n, carry):
        c1, c2 = carry
        start = pl.multiple_of(n * t, t)
        kt = k_ref[pl.ds(start, t), :]
        vt = v_ref[pl.ds(start, t), :]
        bias = bias_ref[jnp.minimum(qi - n, 2)]
        c1 = _softmax_step(_dot_nt(q1, kt) + bias, vt, c1)
        c2 = _softmax_step(_dot_nt(q2, kt) + bias, vt, c2)
        return c1, c2

    (_, l1, a1), (_, l2, a2) = lax.fori_loop(0, qi + 1, body, (_softmax_init(t), _softmax_init(t)))
    lv = lam_ref[...]
    lam = (jnp.exp(jnp.sum(lv[0:1] * lv[1:2], axis=1, keepdims=True))
           - jnp.exp(jnp.sum(lv[2:3] * lv[3:4], axis=1, keepdims=True)) + lambda_init)
    o = a1 / l1 - lam * (a2 / l2)
    o_ref[...] = (_rms(o, subg_ref[...]) * (1.0 - lambda_init)).astype(o_ref.dtype)


def _diff(proj, bias, lam_rows, subg, lambda_init):
    b, s, _ = proj.shape
    t = ATT_TILE
    return pl.pallas_call(
        functools.partial(_diff_kernel, lambda_init=lambda_init),
        grid=(b, DIFF_HEADS, s // t),
        in_specs=[pl.BlockSpec((None, t, HEAD_DIM), lambda bi, h, j: (bi, j, _QC + h)),
                  pl.BlockSpec((None, s, HEAD_DIM), lambda bi, h, j: (bi, 0, _KC + h)),
                  pl.BlockSpec((None, s, HEAD_DIM), lambda bi, h, j: (bi, 0, _VC + h)),
                  pl.BlockSpec((None, 3, t, t), lambda bi, h, j: (h, 0, 0, 0)),
                  pl.BlockSpec(lam_rows.shape, lambda bi, h, j: (0, 0)),
                  pl.BlockSpec(subg.shape, lambda bi, h, j: (0, 0))],
        out_specs=pl.BlockSpec((None, t, HEAD_DIM), lambda bi, h, j: (bi, j, h)),
        out_shape=jax.ShapeDtypeStruct((b, s, DIFF_HEADS * HEAD_DIM), BF16),
        compiler_params=_params(("parallel", "parallel", "parallel")),
        name="diff",
    )(proj, proj, proj, bias, lam_rows, subg)


def _router_kernel(h_ref, g_ref, whi_ref, wlo_ref, xn_ref, route_ref, cnt_ref, carry_ref):
    i = pl.program_id(0)
    tm = h_ref.shape[0]

    @pl.when(i == 0)
    def _():
        carry_ref[...] = jnp.zeros_like(carry_ref)

    xn = _rms(h_ref[...], g_ref[...])
    xn_ref[...] = xn
    x_hi, x_lo = _split_bf16(xn)
    lg = _dot(x_hi, whi_ref[...]) + (_dot(x_hi, wlo_ref[...]) + _dot(x_lo, whi_ref[...]))
    lane = lax.broadcasted_iota(jnp.int32, lg.shape, 1).astype(F32)
    big = float(ROUTE_LANES)

    def first_max(vals):
        best = jnp.max(vals, axis=1, keepdims=True)
        return best, jnp.min(jnp.where(vals == best, lane, big), axis=1, keepdims=True)

    gl = jnp.where(lane < N_GROUPS, lg, -jnp.inf)
    gmax, grp = first_max(gl)
    g_w = 1.0 / jnp.sum(jnp.exp(gl - gmax), axis=1, keepdims=True)
    lo = EXPERT_LANE0 + grp * EXPERTS_PER_GROUP
    el = jnp.where((lane >= lo) & (lane < lo + EXPERTS_PER_GROUP), lg, -jnp.inf)
    m1, i1 = first_max(el)
    m2, i2 = first_max(jnp.where(lane == i1, -jnp.inf, el))
    t2 = jnp.exp(m2 - m1)
    w0 = g_w / (1.0 + t2)
    w1 = w0 * t2
    onehot = jnp.where((lane == i1) | (lane == i2), 1.0, 0.0)
    r = lax.broadcasted_iota(jnp.int32, (tm, tm), 0)
    c = lax.broadcasted_iota(jnp.int32, (tm, tm), 1)
    before = jnp.where(c < r, 1.0, 0.0).astype(BF16)
    counts = _dot(before, onehot.astype(BF16)) + carry_ref[0:1, :]
    rank0 = jnp.sum(jnp.where(lane == i1, counts, 0.0), axis=1, keepdims=True)
    rank1 = jnp.sum(jnp.where(lane == i2, counts, 0.0), axis=1, keepdims=True)
    total = carry_ref[0:1, :] + jnp.sum(onehot, axis=0, keepdims=True)
    carry_ref[0:1, :] = total
    cnt_ref[...] = jnp.broadcast_to(total, cnt_ref.shape)
    rec = jnp.zeros(lg.shape, F32)
    for k, val in enumerate((i1 - EXPERT_LANE0, i2 - EXPERT_LANE0, w0, w1, rank0, rank1)):
        rec = jnp.where(lane == k, val, rec)
    route_ref[...] = rec


def _router(h, g, w_hi, w_lo, *, tm=256):
    t, d = h.shape
    full = lambda a: pl.BlockSpec(a.shape, lambda i: (0, 0))
    return pl.pallas_call(
        _router_kernel,
        grid=(t // tm,),
        in_specs=[pl.BlockSpec((tm, d), lambda i: (i, 0)), full(g), full(w_hi), full(w_lo)],
        out_specs=[pl.BlockSpec((tm, d), lambda i: (i, 0)),
                   pl.BlockSpec((tm, ROUTE_LANES), lambda i: (i, 0)),
                   pl.BlockSpec((8, ROUTE_LANES), lambda i: (0, 0))],
        out_shape=[jax.ShapeDtypeStruct((t, d), F32),
                   jax.ShapeDtypeStruct((t, ROUTE_LANES), F32),
                   jax.ShapeDtypeStruct((8, ROUTE_LANES), F32)],
        scratch_shapes=[pltpu.VMEM((8, ROUTE_LANES), F32)],
        compiler_params=_params(("arbitrary",)),
        name="router",
    )(h, g, w_hi, w_lo)


def _row_copy(src_hbm, dst_vmem, sem, src_row, dst_row):
    return pltpu.make_async_copy(src_hbm.at[pl.ds(src_row, 1)], dst_vmem.at[pl.ds(dst_row, 1)], sem)


def _ffn_kernel(blk_e_ref, tok_ref, nused_ref, xn_hbm, w1_ref, w3_ref, w2_ref, o_ref, xbuf, sem):
    i = pl.program_id(0)
    rows = xbuf.shape[0]

    @pl.when(i < nused_ref[0])
    def _():
        base = i * rows

        def start(r, c):
            _row_copy(xn_hbm, xbuf, sem.at[0], tok_ref[base + r], r).start()
            return c

        def wait(r, c):
            _row_copy(xn_hbm, xbuf, sem.at[0], 0, r).wait()
            return c

        lax.fori_loop(0, rows, start, 0)
        lax.fori_loop(0, rows, wait, 0)
        xb = xbuf[...].astype(BF16)
        a = _dot(xb, w1_ref[...])
        hdn = (a * jax.nn.sigmoid(a) * _dot(xb, w3_ref[...])).astype(BF16)
        o_ref[...] = _dot(hdn, w2_ref[...])

    @pl.when(i >= nused_ref[0])
    def _():
        o_ref[...] = jnp.zeros_like(o_ref)


def _ffn(blk_e, buf_tok, nused, xn, w1, w3, w2):
    t, d = xn.shape
    p_rows = buf_tok.shape[0]
    de = w1.shape[2]
    return pl.pallas_call(
        _ffn_kernel,
        grid_spec=pltpu.PrefetchScalarGridSpec(
            num_scalar_prefetch=3,
            grid=(p_rows // MOE_BLOCK,),
            in_specs=[pl.BlockSpec(memory_space=pl.ANY),
                      pl.BlockSpec((None, d, de), lambda i, be, tk, nu: (be[i], 0, 0)),
                      pl.BlockSpec((None, d, de), lambda i, be, tk, nu: (be[i], 0, 0)),
                      pl.BlockSpec((None, de, d), lambda i, be, tk, nu: (be[i], 0, 0))],
            out_specs=pl.BlockSpec((MOE_BLOCK, d), lambda i, be, tk, nu: (i, 0)),
            scratch_shapes=[pltpu.VMEM((MOE_BLOCK, d), F32), pltpu.SemaphoreType.DMA((1,))]),
        out_shape=jax.ShapeDtypeStruct((p_rows, d), F32),
        compiler_params=_params(("arbitrary",)),
        name="ffn",
    )(blk_e, buf_tok, nused, xn, w1, w3, w2)


def _combine_kernel(d0_ref, d1_ref, h_ref, route_ref, ys_hbm, o_ref, buf, sem):
    i = pl.program_id(0)
    rows = h_ref.shape[0]
    base = i * rows

    def start(r, c):
        _row_copy(ys_hbm, buf.at[0], sem.at[0], d0_ref[base + r], r).start()
        _row_copy(ys_hbm, buf.at[1], sem.at[1], d1_ref[base + r], r).start()
        return c

    def wait(r, c):
        _row_copy(ys_hbm, buf.at[0], sem.at[0], 0, r).wait()
        _row_copy(ys_hbm, buf.at[1], sem.at[1], 0, r).wait()
        return c

    lax.fori_loop(0, rows, start, 0)
    lax.fori_loop(0, rows, wait, 0)
    rec = route_ref[...]
    w0 = rec[:, 2:3]
    w1 = rec[:, 3:4]
    o_ref[...] = h_ref[...] + (w0 * buf[0] + w1 * buf[1])


def _combine(dest0, dest1, h, route, ys, *, tm=256):
    t, d = h.shape
    return pl.pallas_call(
        _combine_kernel,
        grid_spec=pltpu.PrefetchScalarGridSpec(
            num_scalar_prefetch=2,
            grid=(t // tm,),
            in_specs=[pl.BlockSpec((tm, d), lambda i, a, b: (i, 0)),
                      pl.BlockSpec((tm, ROUTE_LANES), lambda i, a, b: (i, 0)),
                      pl.BlockSpec(memory_space=pl.ANY)],
            out_specs=pl.BlockSpec((tm, d), lambda i, a, b: (i, 0)),
            scratch_shapes=[pltpu.VMEM((2, tm, d), F32), pltpu.SemaphoreType.DMA((2,))]),
        out_shape=jax.ShapeDtypeStruct((t, d), F32),
        compiler_params=_params(("arbitrary",)),
        name="combine",
    )(dest0, dest1, h, route, ys)


def _moe(h, g, w_hi, w_lo, w1, w3, w2):
    t = h.shape[0]
    xn, route, cnt = _router(h, g, w_hi, w_lo)
    counts = cnt[0, EXPERT_LANE0:EXPERT_LANE0 + N_EXPERTS].astype(jnp.int32)
    pcounts = ((counts + MOE_BLOCK - 1) // MOE_BLOCK) * MOE_BLOCK
    pends = jnp.cumsum(pcounts)
    pstarts = pends - pcounts
    e0 = route[:, 0].astype(jnp.int32)
    e1 = route[:, 1].astype(jnp.int32)
    dest0 = pstarts[e0] + route[:, 4].astype(jnp.int32)
    dest1 = pstarts[e1] + route[:, 5].astype(jnp.int32)
    p_rows = 2 * t + N_EXPERTS * MOE_BLOCK
    tok = jnp.arange(t, dtype=jnp.int32)
    buf_tok = jnp.zeros((p_rows,), jnp.int32).at[dest0].set(tok).at[dest1].set(tok)
    nblk = p_rows // MOE_BLOCK
    blk_e = jnp.minimum(jnp.searchsorted(pends, jnp.arange(nblk, dtype=jnp.int32) * MOE_BLOCK, side='right'),
                        N_EXPERTS - 1).astype(jnp.int32)
    nused = (pends[-1:] // MOE_BLOCK).astype(jnp.int32)
    ys = _ffn(blk_e, buf_tok, nused, xn, w1, w3, w2)
    return _combine(dest0, dest1, h, route, ys)


def kernel(x, p, rel_bias, g_final, g_mix, w_in, swa_sinks, lam_q1, lam_k1, lam_q2, lam_k2, diff_subln, w_out,
           g_ffn, w_group, w_expert, w1, w3, w2, g_ple, w_ple_gate, w_ple_proj):
    b, s, d = x.shape
    depth = w_in.shape[0]
    t = b * s
    bias_a = _swa_bias(rel_bias[:, :SWA_HEADS])
    bias_b = _tile_bias(rel_bias[:, SWA_HEADS:SWA_HEADS + MOBA_HEADS])
    bias_c = _tile_bias(rel_bias[:, SWA_HEADS + MOBA_HEADS:])
    w_in, w_out, w1, w3, w2, w_ple_gate, w_ple_proj = (
        a.astype(BF16) for a in (w_in, w_out, w1, w3, w2, w_ple_gate, w_ple_proj))
    w_route = jnp.concatenate(
        [w_group, w_expert, jnp.zeros((depth, d, ROUTE_LANES - N_GROUPS - N_EXPERTS), F32)], axis=-1)
    w_route_hi = w_route.astype(BF16)
    w_route_lo = (w_route - w_route_hi.astype(F32)).astype(BF16)
    lam_rows = jnp.stack([lam_q1, lam_k1, lam_q2, lam_k2], axis=1).astype(F32)
    lam_rows = jnp.pad(lam_rows, ((0, 0), (0, 4), (0, HEAD_DIM - DIFF_QK_DIM)))
    row = lambda a: a.reshape(1, -1).astype(F32)

    h = x.reshape(t, d)
    for i in range(depth):
        proj = _inproj(h, row(g_mix[i]), w_in[i]).reshape(b, s, IN_WIDTH)
        ya = _swa(proj, bias_a, swa_sinks[i].astype(F32))
        yb = _moba(proj, bias_b)
        lambda_init = 0.8 - 0.6 * math.exp(-0.3 * i)
        yc = _diff(proj, bias_c, lam_rows[i], row(diff_subln[i]), lambda_init)
        h = _outproj(ya.reshape(t, -1), yb.reshape(t, -1), yc.reshape(t, -1), w_out[i], h)
        h = _moe(h, row(g_ffn[i]), w_route_hi[i], w_route_lo[i], w1[i], w3[i], w2[i])
        h = _ple(h, row(g_ple[i]), w_ple_gate[i], p[i].reshape(t, PLE_DIM), w_ple_proj[i], row(g_final),
                 final=(i == depth - 1))
    return h.reshape(b, s, d)
```

```python
import functools
import math

import jax
import jax.numpy as jnp
import numpy as np
from jax import lax
from jax.experimental import pallas as pl
from jax.experimental.pallas import tpu as pltpu

D_MODEL = 2048
HEAD_DIM = 128
SWA_HEADS = 6
SWA_KV_HEADS = 2
SWA_GROUP = SWA_HEADS // SWA_KV_HEADS
WINDOW = 128
MOBA_HEADS = 4
MOBA_BLOCK = 256
MOBA_TOPK = 3
DIFF_HEADS = 6
DIFF_QK_DIM = HEAD_DIM // 2
IN_WIDTH = 5120
REL_BUCKETS = 32
REL_EXACT = REL_BUCKETS // 2
REL_MAX_DIST = 128
N_GROUPS = 4
EXPERTS_PER_GROUP = 8
N_EXPERTS = N_GROUPS * EXPERTS_PER_GROUP
D_EXPERT = 512
MOE_BLOCK = 256
PLE_DIM = 256
EPS = 1e-6
NEG = -1e30

_QB, _KB, _VB = 0, 4, 8
_QC, _KC, _VC = 12, 18, 24
_QA, _KA, _VA = 30, 36, 38
_REF_GROUP_COLS = {"a": (0, 1280), "b": (1280, 2816), "c": (2816, 5120)}

ATT_TILE = 256
ATT_KTILE = 2 * ATT_TILE
ATT_BIAS_CLASSES = 4
LOG2E = math.log2(math.e)
DIFF_HEADS_PER_STEP = 3
ROUTE_LANES = 128
EXPERT_LANE0 = N_GROUPS
VMEM_LIMIT_BYTES = 56 * 1024 * 1024

F32 = jnp.float32
BF16 = jnp.bfloat16


def _params(semantics):
    return pltpu.CompilerParams(dimension_semantics=semantics, vmem_limit_bytes=VMEM_LIMIT_BYTES)


def _rms(x, g):
    ms = jnp.mean(x * x, axis=-1, keepdims=True)
    return x * lax.rsqrt(ms + EPS) * g


def _dot(a, b):
    return jnp.dot(a, b, preferred_element_type=F32)


def _dot_nt(a, b):
    return lax.dot_general(a, b, (((1,), (1,)), ((), ())), preferred_element_type=F32)


def _split_bf16(x):
    hi = x.astype(BF16)
    lo = (x - hi.astype(F32)).astype(BF16)
    return hi, lo


def _bucket_table(max_dist):
    n = np.arange(max_dist)
    nf = np.maximum(n, 1).astype(np.float32)
    large = REL_EXACT + (np.log(nf / np.float32(REL_EXACT)) / np.float32(math.log(REL_MAX_DIST / REL_EXACT))
                         * np.float32(REL_BUCKETS - REL_EXACT)).astype(np.int32)
    large = np.minimum(large, REL_BUCKETS - 1)
    return np.where(n < REL_EXACT, n, large).astype(np.int32)


def _swa_bias(bias_a):
    qi = np.arange(WINDOW)[:, None]
    pj = np.arange(2 * WINDOW)[None, :]
    dist = qi + WINDOW - pj
    inside = (dist >= 0) & (dist < WINDOW)
    bucket = _bucket_table(2 * WINDOW)[np.maximum(dist, 0)]
    vals = jnp.transpose(bias_a[bucket], (2, 0, 1))
    return jnp.where(inside[None], vals, NEG).astype(F32)


def _tile_bias(bias_h):
    i = np.arange(ATT_TILE)[:, None]
    j = np.arange(ATT_KTILE)[None, :]
    table = _bucket_table(ATT_BIAS_CLASSES * ATT_TILE)
    tiles = []
    for c in range(ATT_BIAS_CLASSES):
        dist = c * ATT_TILE + i - j
        vals = jnp.transpose(bias_h[table[np.maximum(dist, 0)]], (2, 0, 1))
        tiles.append(jnp.where((dist >= 0)[None], vals, NEG))
    last = (ATT_BIAS_CLASSES - 1) * ATT_TILE + i - j
    assert int(table[last.min()]) == REL_BUCKETS - 1
    return (jnp.stack(tiles, axis=1) * LOG2E).astype(F32)


def _inproj_kernel(x_ref, g_ref, w_ref, o_ref, xn_ref):
    @pl.when(pl.program_id(1) == 0)
    def _():
        xn_ref[...] = _rms(x_ref[...], g_ref[...]).astype(BF16)

    o_ref[...] = _dot(xn_ref[...], w_ref[...]).astype(o_ref.dtype)


def _inproj(h, g, w, *, tm=512, tn=1280):
    t, d = h.shape
    n = w.shape[1]
    return pl.pallas_call(
        _inproj_kernel,
        grid=(t // tm, n // tn),
        in_specs=[pl.BlockSpec((tm, d), lambda i, j: (i, 0)),
                  pl.BlockSpec((1, d), lambda i, j: (0, 0)),
                  pl.BlockSpec((d, tn), lambda i, j: (0, j))],
        out_specs=pl.BlockSpec((tm, tn), lambda i, j: (i, j)),
        out_shape=jax.ShapeDtypeStruct((t, n), BF16),
        scratch_shapes=[pltpu.VMEM((tm, d), BF16)],
        compiler_params=_params(("parallel", "arbitrary")),
        name="inproj",
    )(h, g, w)


def _outproj_kernel(ya_ref, yb_ref, yc_ref, w_ref, h_ref, o_ref):
    na, nb = ya_ref.shape[1], yb_ref.shape[1]
    y = _dot(ya_ref[...], w_ref[0:na, :])
    y += _dot(yb_ref[...], w_ref[na:na + nb, :])
    y += _dot(yc_ref[...], w_ref[na + nb:, :])
    o_ref[...] = h_ref[...] + y


def _outproj(ya, yb, yc, w, h, *, tm=256):
    t, d = h.shape
    row = lambda width: pl.BlockSpec((tm, width), lambda i: (i, 0))
    return pl.pallas_call(
        _outproj_kernel,
        grid=(t // tm,),
        in_specs=[row(ya.shape[1]), row(yb.shape[1]), row(yc.shape[1]),
                  pl.BlockSpec(w.shape, lambda i: (0, 0)), row(d)],
        out_specs=row(d),
        out_shape=jax.ShapeDtypeStruct((t, d), F32),
        compiler_params=_params(("parallel",)),
        name="outproj",
    )(ya, yb, yc, w, h)


def _ple_kernel(h_ref, g_ref, wg_ref, p_ref, wp_ref, gf_ref, o_ref, *, final):
    h = h_ref[...]
    u = _rms(h, g_ref[...]).astype(BF16)
    gate = jax.nn.sigmoid(_dot(u, wg_ref[...]))
    o = h + gate * _dot(p_ref[...].astype(BF16), wp_ref[...])
    if final:
        o = _rms(o, gf_ref[...])
    o_ref[...] = o


def _ple(h, g, wg, p, wp, g_final, *, final, tm=256):
    t, d = h.shape
    row = lambda width: pl.BlockSpec((tm, width), lambda i: (i, 0))
    full = lambda a: pl.BlockSpec(a.shape, lambda i: (0, 0))
    return pl.pallas_call(
        functools.partial(_ple_kernel, final=final),
        grid=(t // tm,),
        in_specs=[row(d), full(g), full(wg), row(p.shape[1]), full(wp), full(g_final)],
        out_specs=row(d),
        out_shape=jax.ShapeDtypeStruct((t, d), F32),
        compiler_params=_params(("parallel",)),
        name="ple",
    )(h, g, wg, p, wp, g_final)


def _swa_kernel(sink_ref, q_ref, kp_ref, kc_ref, vp_ref, vc_ref, bias_ref, o_ref):
    n = pl.program_id(1)
    w = WINDOW
    scale = HEAD_DIM ** -0.5
    col = lax.broadcasted_iota(jnp.int32, (w, 2 * w), 1)
    prev_pen = jnp.where(col < w, jnp.where(n == 0, NEG, 0.0).astype(F32), 0.0)
    for kv in range(SWA_KV_HEADS):
        cols = slice(kv * HEAD_DIM, (kv + 1) * HEAD_DIM)
        kcat = jnp.concatenate([kp_ref[:, cols], kc_ref[:, cols]], axis=0)
        vcat = jnp.concatenate([vp_ref[:, cols], vc_ref[:, cols]], axis=0)
        for g in range(SWA_GROUP):
            hd = kv * SWA_GROUP + g
            hcols = slice(hd * HEAD_DIM, (hd + 1) * HEAD_DIM)
            s = _dot_nt(q_ref[:, hcols], kcat) * scale + bias_ref[hd] + prev_pen
            sink = sink_ref[hd]
            m = jnp.maximum(jnp.max(s, axis=1, keepdims=True), sink)
            p = jnp.exp(s - m)
            denom = jnp.sum(p, axis=1, keepdims=True) + jnp.exp(sink - m)
            o = _dot(p.astype(BF16), vcat) / denom
            o_ref[:, hcols] = o.astype(o_ref.dtype)


def _swa(proj, bias, sinks):
    b, s, _ = proj.shape
    w = WINDOW
    qw, kw = SWA_HEADS * HEAD_DIM, SWA_KV_HEADS * HEAD_DIM
    prev = lambda blk: pl.BlockSpec((None, w, kw), lambda bi, n: (bi, jnp.maximum(n - 1, 0), blk))
    cur = lambda blk: pl.BlockSpec((None, w, kw), lambda bi, n: (bi, n, blk))
    kblk, vblk = (_KA * HEAD_DIM) // kw, (_VA * HEAD_DIM) // kw
    qblk = (_QA * HEAD_DIM) // qw
    return pl.pallas_call(
        _swa_kernel,
        grid=(b, s // w),
        in_specs=[pl.BlockSpec(memory_space=pltpu.SMEM),
                  pl.BlockSpec((None, w, qw), lambda bi, n: (bi, n, qblk)),
                  prev(kblk), cur(kblk), prev(vblk), cur(vblk),
                  pl.BlockSpec(bias.shape, lambda bi, n: (0, 0, 0))],
        out_specs=pl.BlockSpec((None, w, qw), lambda bi, n: (bi, n, 0)),
        out_shape=jax.ShapeDtypeStruct((b, s, qw), BF16),
        compiler_params=_params(("parallel", "parallel")),
        name="swa",
    )(sinks, proj, proj, proj, proj, proj, bias)


def _softmax_step(s, vt, m_ref, l_ref, acc_ref):
    reps = s.shape[1] // HEAD_DIM
    m_prev = m_ref[...]
    m_next = jnp.maximum(m_prev, jnp.max(s, axis=1, keepdims=True))
    alpha = jnp.exp2(m_prev - m_next)
    p = jnp.exp2(s - jnp.concatenate([m_next] * reps, axis=1))
    l_ref[...] = alpha * l_ref[...] + jnp.sum(p, axis=1, keepdims=True)
    acc_ref[...] = alpha * acc_ref[...] + _dot(p.astype(BF16), vt)
    m_ref[...] = m_next


def _softmax_init(m_ref, l_ref, acc_ref):
    m_ref[...] = jnp.full(m_ref.shape, NEG, F32)
    l_ref[...] = jnp.zeros(l_ref.shape, F32)
    acc_ref[...] = jnp.zeros(acc_ref.shape, F32)


def _moba_kernel(q_ref, k_ref, v_ref, bias_ref, o_ref, km_ref, pen_ref, m_ref, l_ref, acc_ref):
    j = pl.program_id(1)
    t, tk = ATT_TILE, ATT_KTILE
    blocks_per_tile = tk // t
    nb = k_ref.shape[0] // t
    heads = q_ref.shape[1] // HEAD_DIM
    scale = HEAD_DIM ** -0.5
    hcols = [slice(h * HEAD_DIM, (h + 1) * HEAD_DIM) for h in range(heads)]

    @pl.when(j == 0)
    def _():
        km_ref[...] = jnp.zeros_like(km_ref)
        for n in range(nb):
            kmean = jnp.mean(k_ref[n * t:(n + 1) * t, :].astype(F32), axis=0, keepdims=True)
            for h in range(heads):
                km_ref[h, n:n + 1, :] = kmean[:, hcols[h]]

    lane = lax.broadcasted_iota(jnp.int32, (t, ROUTE_LANES), 1)
    lane_f = lane.astype(F32)
    past = lane < j
    for h in range(heads):
        km_hi, km_lo = _split_bf16(km_ref[h])
        q = q_ref[:, hcols[h]]
        gate = _dot_nt(q, km_hi) + _dot_nt(q, km_lo)
        gate = jnp.where(past, gate, -jnp.inf)
        pen = jnp.full(gate.shape, NEG, F32)
        for _ in range(MOBA_TOPK):
            best = jnp.max(gate, axis=1, keepdims=True)
            first = jnp.min(jnp.where(gate == best, lane_f, float(ROUTE_LANES)), axis=1, keepdims=True)
            pick = (lane_f == first) & past
            pen = jnp.where(pick, 0.0, pen)
            gate = jnp.where(pick, -jnp.inf, gate)
        pen_ref[h] = jnp.where(lane == j, 0.0, pen)

    _softmax_init(m_ref, l_ref, acc_ref)

    def body(n, c):
        start = pl.multiple_of(n * tk, tk)
        cls = jnp.minimum(j - blocks_per_tile * n, ATT_BIAS_CLASSES - 1)
        for h in range(heads):
            kt = k_ref[pl.ds(start, tk), hcols[h]]
            vt = v_ref[pl.ds(start, tk), hcols[h]]
            pen = pen_ref[h]
            pens = [jnp.broadcast_to(jnp.sum(jnp.where(lane == blocks_per_tile * n + i, pen, 0.0), axis=1,
                                             keepdims=True), (t, t)) for i in range(blocks_per_tile)]
            s = _dot_nt(q_ref[:, hcols[h]], kt) * (scale * LOG2E) + bias_ref[h, cls]
            s = s + jnp.concatenate(pens, axis=1)
            _softmax_step(s, vt, m_ref.at[h], l_ref.at[h], acc_ref.at[h])
        return c

    lax.fori_loop(0, j // blocks_per_tile + 1, body, 0)
    for h in range(heads):
        o_ref[:, hcols[h]] = (acc_ref[h] / l_ref[h]).astype(o_ref.dtype)


def _moba(proj, bias):
    b, s, _ = proj.shape
    t = ATT_TILE
    heads = MOBA_HEADS
    width = heads * HEAD_DIM
    blk = lambda off: (off * HEAD_DIM) // width
    return pl.pallas_call(
        _moba_kernel,
        grid=(b, s // t),
        in_specs=[pl.BlockSpec((None, t, width), lambda bi, j: (bi, j, blk(_QB))),
                  pl.BlockSpec((None, s, width), lambda bi, j: (bi, 0, blk(_KB))),
                  pl.BlockSpec((None, s, width), lambda bi, j: (bi, 0, blk(_VB))),
                  pl.BlockSpec(bias.shape, lambda bi, j: (0, 0, 0, 0))],
        out_specs=pl.BlockSpec((None, t, width), lambda bi, j: (bi, j, 0)),
        out_shape=jax.ShapeDtypeStruct((b, s, width), BF16),
        scratch_shapes=[pltpu.VMEM((heads, ROUTE_LANES, HEAD_DIM), F32), pltpu.VMEM((heads, t, ROUTE_LANES), F32)]
        + [pltpu.VMEM((heads, t, HEAD_DIM), F32)] * 3,
        compiler_params=_params(("arbitrary", "arbitrary")),
        name="moba",
    )(proj, proj, proj, bias)


def _diff_kernel(q_ref, k_ref, v_ref, bias_ref, lam_ref, subg_ref, o_ref, m_ref, l_ref, acc_ref, *, lambda_init):
    qi = pl.program_id(2)
    t, tk = ATT_TILE, ATT_KTILE
    heads = q_ref.shape[1] // HEAD_DIM
    hcols = [slice(h * HEAD_DIM, (h + 1) * HEAD_DIM) for h in range(heads)]
    lane = lax.broadcasted_iota(jnp.int32, (t, HEAD_DIM), 1)
    qs = []
    for h in range(heads):
        q = (q_ref[:, hcols[h]].astype(F32) * (DIFF_QK_DIM ** -0.5)).astype(BF16)
        zero = jnp.zeros_like(q)
        qs.append(jnp.where(lane < DIFF_QK_DIM, q, zero))
        qs.append(jnp.where(lane >= DIFF_QK_DIM, q, zero))

    _softmax_init(m_ref, l_ref, acc_ref)

    def body(n, c):
        start = pl.multiple_of(n * tk, tk)
        cls = jnp.minimum(qi - (tk // t) * n, ATT_BIAS_CLASSES - 1)
        for h in range(heads):
            kt = k_ref[pl.ds(start, tk), hcols[h]]
            vt = v_ref[pl.ds(start, tk), hcols[h]]
            bias = bias_ref[h, cls]
            for half in range(2):
                st = 2 * h + half
                _softmax_step(_dot_nt(qs[st], kt) * LOG2E + bias, vt, m_ref.at[st], l_ref.at[st], acc_ref.at[st])
        return c

    lax.fori_loop(0, qi // (tk // t) + 1, body, 0)
    lv = lam_ref[...]
    lam = (jnp.exp(jnp.sum(lv[0:1] * lv[1:2], axis=1, keepdims=True))
           - jnp.exp(jnp.sum(lv[2:3] * lv[3:4], axis=1, keepdims=True)) + lambda_init)
    for h in range(heads):
        o = acc_ref[2 * h] / l_ref[2 * h] - lam * (acc_ref[2 * h + 1] / l_ref[2 * h + 1])
        o_ref[:, hcols[h]] = (_rms(o, subg_ref[...]) * (1.0 - lambda_init)).astype(o_ref.dtype)


def _diff(proj, bias, lam_rows, subg, lambda_init):
    b, s, _ = proj.shape
    t = ATT_TILE
    heads = DIFF_HEADS_PER_STEP
    width = heads * HEAD_DIM
    blk = lambda off: (off * HEAD_DIM) // width
    return pl.pallas_call(
        functools.partial(_diff_kernel, lambda_init=lambda_init),
        grid=(b, DIFF_HEADS // heads, s // t),
        in_specs=[pl.BlockSpec((None, t, width), lambda bi, h, j: (bi, j, blk(_QC) + h)),
                  pl.BlockSpec((None, s, width), lambda bi, h, j: (bi, 0, blk(_KC) + h)),
                  pl.BlockSpec((None, s, width), lambda bi, h, j: (bi, 0, blk(_VC) + h)),
                  pl.BlockSpec((heads,) + bias.shape[1:], lambda bi, h, j: (h, 0, 0, 0)),
                  pl.BlockSpec(lam_rows.shape, lambda bi, h, j: (0, 0)),
                  pl.BlockSpec(subg.shape, lambda bi, h, j: (0, 0))],
        out_specs=pl.BlockSpec((None, t, width), lambda bi, h, j: (bi, j, h)),
        out_shape=jax.ShapeDtypeStruct((b, s, DIFF_HEADS * HEAD_DIM), BF16),
        scratch_shapes=[pltpu.VMEM((2 * heads, t, HEAD_DIM), F32)] * 3,
        compiler_params=_params(("parallel", "parallel", "parallel")),
        name="diff",
    )(proj, proj, proj, bias, lam_rows, subg)


def _router_kernel(h_ref, g_ref, whi_ref, wlo_ref, xn_ref, route_ref, cnt_ref, carry_ref):
    i = pl.program_id(0)
    tm = h_ref.shape[0]

    @pl.when(i == 0)
    def _():
        carry_ref[...] = jnp.zeros_like(carry_ref)

    xn = _rms(h_ref[...], g_ref[...])
    xn_ref[...] = xn
    x_hi, x_lo = _split_bf16(xn)
    lg = _dot(x_hi, whi_ref[...]) + (_dot(x_hi, wlo_ref[...]) + _dot(x_lo, whi_ref[...]))
    lane = lax.broadcasted_iota(jnp.int32, lg.shape, 1).astype(F32)
    big = float(ROUTE_LANES)

    def first_max(vals):
        best = jnp.max(vals, axis=1, keepdims=True)
        return best, jnp.min(jnp.where(vals == best, lane, big), axis=1, keepdims=True)

    gl = jnp.where(lane < N_GROUPS, lg, -jnp.inf)
    gmax, grp = first_max(gl)
    g_w = 1.0 / jnp.sum(jnp.exp(gl - gmax), axis=1, keepdims=True)
    lo = EXPERT_LANE0 + grp * EXPERTS_PER_GROUP
    el = jnp.where((lane >= lo) & (lane < lo + EXPERTS_PER_GROUP), lg, -jnp.inf)
    m1, i1 = first_max(el)
    m2, i2 = first_max(jnp.where(lane == i1, -jnp.inf, el))
    t2 = jnp.exp(m2 - m1)
    w0 = g_w / (1.0 + t2)
    w1 = w0 * t2
    onehot = jnp.where((lane == i1) | (lane == i2), 1.0, 0.0)
    r = lax.broadcasted_iota(jnp.int32, (tm, tm), 0)
    c = lax.broadcasted_iota(jnp.int32, (tm, tm), 1)
    before = jnp.where(c < r, 1.0, 0.0).astype(BF16)
    counts = _dot(before, onehot.astype(BF16)) + carry_ref[0:1, :]
    rank0 = jnp.sum(jnp.where(lane == i1, counts, 0.0), axis=1, keepdims=True)
    rank1 = jnp.sum(jnp.where(lane == i2, counts, 0.0), axis=1, keepdims=True)
    total = carry_ref[0:1, :] + jnp.sum(onehot, axis=0, keepdims=True)
    carry_ref[0:1, :] = total
    cnt_ref[...] = jnp.broadcast_to(total, cnt_ref.shape)
    rec = jnp.zeros(lg.shape, F32)
    for k, val in enumerate((i1 - EXPERT_LANE0, i2 - EXPERT_LANE0, w0, w1, rank0, rank1)):
        rec = jnp.where(lane == k, val, rec)
    route_ref[...] = rec


def _router(h, g, w_hi, w_lo, *, tm=256):
    t, d = h.shape
    full = lambda a: pl.BlockSpec(a.shape, lambda i: (0, 0))
    return pl.pallas_call(
        _router_kernel,
        grid=(t // tm,),
        in_specs=[pl.BlockSpec((tm, d), lambda i: (i, 0)), full(g), full(w_hi), full(w_lo)],
        out_specs=[pl.BlockSpec((tm, d), lambda i: (i, 0)),
                   pl.BlockSpec((tm, ROUTE_LANES), lambda i: (i, 0)),
                   pl.BlockSpec((8, ROUTE_LANES), lambda i: (0, 0))],
        out_shape=[jax.ShapeDtypeStruct((t, d), F32),
                   jax.ShapeDtypeStruct((t, ROUTE_LANES), F32),
                   jax.ShapeDtypeStruct((8, ROUTE_LANES), F32)],
        scratch_shapes=[pltpu.VMEM((8, ROUTE_LANES), F32)],
        compiler_params=_params(("arbitrary",)),
        name="router",
    )(h, g, w_hi, w_lo)


def _row_copy(src_hbm, dst_vmem, sem, src_row, dst_row):
    return pltpu.make_async_copy(src_hbm.at[pl.ds(src_row, 1)], dst_vmem.at[pl.ds(dst_row, 1)], sem)


def _ffn_kernel(blk_e_ref, tok_ref, nused_ref, xn_hbm, w1_ref, w3_ref, w2_ref, o_ref, xbuf, sem):
    i = pl.program_id(0)
    rows = xbuf.shape[0]

    @pl.when(i < nused_ref[0])
    def _():
        base = i * rows

        def start(r, c):
            _row_copy(xn_hbm, xbuf, sem.at[0], tok_ref[base + r], r).start()
            return c

        def wait(r, c):
            _row_copy(xn_hbm, xbuf, sem.at[0], 0, r).wait()
            return c

        lax.fori_loop(0, rows, start, 0)
        lax.fori_loop(0, rows, wait, 0)
        xb = xbuf[...].astype(BF16)
        a = _dot(xb, w1_ref[...])
        hdn = (a * jax.nn.sigmoid(a) * _dot(xb, w3_ref[...])).astype(BF16)
        o_ref[...] = _dot(hdn, w2_ref[...])

    @pl.when(i >= nused_ref[0])
    def _():
        o_ref[...] = jnp.zeros_like(o_ref)


def _ffn(blk_e, buf_tok, nused, xn, w1, w3, w2):
    t, d = xn.shape
    p_rows = buf_tok.shape[0]
    de = w1.shape[2]
    return pl.pallas_call(
        _ffn_kernel,
        grid_spec=pltpu.PrefetchScalarGridSpec(
            num_scalar_prefetch=3,
            grid=(p_rows // MOE_BLOCK,),
            in_specs=[pl.BlockSpec(memory_space=pl.ANY),
                      pl.BlockSpec((None, d, de), lambda i, be, tk, nu: (be[i], 0, 0)),
                      pl.BlockSpec((None, d, de), lambda i, be, tk, nu: (be[i], 0, 0)),
                      pl.BlockSpec((None, de, d), lambda i, be, tk, nu: (be[i], 0, 0))],
            out_specs=pl.BlockSpec((MOE_BLOCK, d), lambda i, be, tk, nu: (i, 0)),
            scratch_shapes=[pltpu.VMEM((MOE_BLOCK, d), F32), pltpu.SemaphoreType.DMA((1,))]),
        out_shape=jax.ShapeDtypeStruct((p_rows, d), F32),
        compiler_params=_params(("arbitrary",)),
        name="ffn",
    )(blk_e, buf_tok, nused, xn, w1, w3, w2)


def _combine_kernel(d0_ref, d1_ref, h_ref, route_ref, ys_hbm, o_ref, buf, sem):
    i = pl.program_id(0)
    rows = h_ref.shape[0]
    base = i * rows

    def start(r, c):
        _row_copy(ys_hbm, buf.at[0], sem.at[0], d0_ref[base + r], r).start()
        _row_copy(ys_hbm, buf.at[1], sem.at[1], d1_ref[base + r], r).start()
        return c

    def wait(r, c):
        _row_copy(ys_hbm, buf.at[0], sem.at[0], 0, r).wait()
        _row_copy(ys_hbm, buf.at[1], sem.at[1], 0, r).wait()
        return c

    lax.fori_loop(0, rows, start, 0)
    lax.fori_loop(0, rows, wait, 0)
    rec = route_ref[...]
    w0 = rec[:, 2:3]
    w1 = rec[:, 3:4]
    o_ref[...] = h_ref[...] + (w0 * buf[0] + w1 * buf[1])


def _combine(dest0, dest1, h, route, ys, *, tm=256):
    t, d = h.shape
    return pl.pallas_call(
        _combine_kernel,
        grid_spec=pltpu.PrefetchScalarGridSpec(
            num_scalar_prefetch=2,
            grid=(t // tm,),
            in_specs=[pl.BlockSpec((tm, d), lambda i, a, b: (i, 0)),
                      pl.BlockSpec((tm, ROUTE_LANES), lambda i, a, b: (i, 0)),
                      pl.BlockSpec(memory_space=pl.ANY)],
            out_specs=pl.BlockSpec((tm, d), lambda i, a, b: (i, 0)),
            scratch_shapes=[pltpu.VMEM((2, tm, d), F32), pltpu.SemaphoreType.DMA((2,))]),
        out_shape=jax.ShapeDtypeStruct((t, d), F32),
        compiler_params=_params(("arbitrary",)),
        name="combine",
    )(dest0, dest1, h, route, ys)


def _moe(h, g, w_hi, w_lo, w1, w3, w2):
    t = h.shape[0]
    xn, route, cnt = _router(h, g, w_hi, w_lo)
    counts = cnt[0, EXPERT_LANE0:EXPERT_LANE0 + N_EXPERTS].astype(jnp.int32)
    pcounts = ((counts + MOE_BLOCK - 1) // MOE_BLOCK) * MOE_BLOCK
    pends = jnp.cumsum(pcounts)
    pstarts = pends - pcounts
    e0 = route[:, 0].astype(jnp.int32)
    e1 = route[:, 1].astype(jnp.int32)
    dest0 = pstarts[e0] + route[:, 4].astype(jnp.int32)
    dest1 = pstarts[e1] + route[:, 5].astype(jnp.int32)
    p_rows = 2 * t + N_EXPERTS * MOE_BLOCK
    tok = jnp.arange(t, dtype=jnp.int32)
    buf_tok = jnp.zeros((p_rows,), jnp.int32).at[dest0].set(tok).at[dest1].set(tok)
    nblk = p_rows // MOE_BLOCK
    blk_e = jnp.minimum(jnp.searchsorted(pends, jnp.arange(nblk, dtype=jnp.int32) * MOE_BLOCK, side='right'),
                        N_EXPERTS - 1).astype(jnp.int32)
    nused = (pends[-1:] // MOE_BLOCK).astype(jnp.int32)
    ys = _ffn(blk_e, buf_tok, nused, xn, w1, w3, w2)
    return _combine(dest0, dest1, h, route, ys)


def kernel(x, p, rel_bias, g_final, g_mix, w_in, swa_sinks, lam_q1, lam_k1, lam_q2, lam_k2, diff_subln, w_out,
           g_ffn, w_group, w_expert, w1, w3, w2, g_ple, w_ple_gate, w_ple_proj):
    b, s, d = x.shape
    depth = w_in.shape[0]
    t = b * s
    bias_a = _swa_bias(rel_bias[:, :SWA_HEADS])
    bias_b = _tile_bias(rel_bias[:, SWA_HEADS:SWA_HEADS + MOBA_HEADS])
    bias_c = _tile_bias(rel_bias[:, SWA_HEADS + MOBA_HEADS:])
    w_in = jnp.concatenate([w_in[..., lo:hi] for lo, hi in (_REF_GROUP_COLS[g] for g in "bca")], axis=-1)
    w_in, w_out, w1, w3, w2, w_ple_gate, w_ple_proj = (
        a.astype(BF16) for a in (w_in, w_out, w1, w3, w2, w_ple_gate, w_ple_proj))
    w_route = jnp.concatenate(
        [w_group, w_expert, jnp.zeros((depth, d, ROUTE_LANES - N_GROUPS - N_EXPERTS), F32)], axis=-1)
    w_route_hi = w_route.astype(BF16)
    w_route_lo = (w_route - w_route_hi.astype(F32)).astype(BF16)
    lam_rows = jnp.stack([lam_q1, lam_k1, lam_q2, lam_k2], axis=1).astype(F32)
    lam_rows = jnp.pad(lam_rows, ((0, 0), (0, 4), (0, HEAD_DIM - DIFF_QK_DIM)))
    row = lambda a: a.reshape(1, -1).astype(F32)

    h = x.reshape(t, d)
    for i in range(depth):
        proj = _inproj(h, row(g_mix[i]), w_in[i]).reshape(b, s, IN_WIDTH)
        ya = _swa(proj, bias_a, swa_sinks[i].astype(F32))
        yb = _moba(proj, bias_b)
        lambda_init = 0.8 - 0.6 * math.exp(-0.3 * i)
        yc = _diff(proj, bias_c, lam_rows[i], row(diff_subln[i]), lambda_init)
        h = _outproj(ya.reshape(t, -1), yb.reshape(t, -1), yc.reshape(t, -1), w_out[i], h)
        h = _moe(h, row(g_ffn[i]), w_route_hi[i], w_route_lo[i], w1[i], w3[i], w2[i])
        h = _ple(h, row(g_ple[i]), w_ple_gate[i], p[i].reshape(t, PLE_DIM), w_ple_proj[i], row(g_final),
                 final=(i == depth - 1))
    return h.reshape(b, s, d)
```

```python
import functools
import math

import jax
import jax.numpy as jnp
import numpy as np
from jax import lax
from jax.experimental import pallas as pl
from jax.experimental.pallas import tpu as pltpu

D_MODEL = 2048
HEAD_DIM = 128
SWA_HEADS = 6
SWA_KV_HEADS = 2
SWA_GROUP = SWA_HEADS // SWA_KV_HEADS
WINDOW = 128
MOBA_HEADS = 4
MOBA_BLOCK = 256
MOBA_TOPK = 3
DIFF_HEADS = 6
DIFF_QK_DIM = HEAD_DIM // 2
IN_WIDTH = 5120
REL_BUCKETS = 32
REL_EXACT = REL_BUCKETS // 2
REL_MAX_DIST = 128
N_GROUPS = 4
EXPERTS_PER_GROUP = 8
N_EXPERTS = N_GROUPS * EXPERTS_PER_GROUP
D_EXPERT = 512
MOE_BLOCK = 256
PLE_DIM = 256
EPS = 1e-6
NEG = -1e30

_QB, _KB, _VB = 0, 4, 8
_QC, _KC, _VC = 12, 18, 24
_QA, _KA, _VA = 30, 36, 38
_REF_GROUP_COLS = {"a": (0, 1280), "b": (1280, 2816), "c": (2816, 5120)}

ATT_TILE = 256
ATT_KTILE = 2 * ATT_TILE
ATT_BIAS_CLASSES = 4
LOG2E = math.log2(math.e)
DIFF_HEADS_PER_STEP = 3
ROUTE_LANES = 128
EXPERT_LANE0 = N_GROUPS
VMEM_LIMIT_BYTES = 56 * 1024 * 1024

F32 = jnp.float32
BF16 = jnp.bfloat16


def _params(semantics):
    return pltpu.CompilerParams(dimension_semantics=semantics, vmem_limit_bytes=VMEM_LIMIT_BYTES)


def _rms(x, g):
    ms = jnp.mean(x * x, axis=-1, keepdims=True)
    return x * lax.rsqrt(ms + EPS) * g


def _dot(a, b):
    return jnp.dot(a, b, preferred_element_type=F32)


def _dot_nt(a, b):
    return lax.dot_general(a, b, (((1,), (1,)), ((), ())), preferred_element_type=F32)


def _split_bf16(x):
    hi = x.astype(BF16)
    lo = (x - hi.astype(F32)).astype(BF16)
    return hi, lo


def _bucket_table(max_dist):
    n = np.arange(max_dist)
    nf = np.maximum(n, 1).astype(np.float32)
    large = REL_EXACT + (np.log(nf / np.float32(REL_EXACT)) / np.float32(math.log(REL_MAX_DIST / REL_EXACT))
                         * np.float32(REL_BUCKETS - REL_EXACT)).astype(np.int32)
    large = np.minimum(large, REL_BUCKETS - 1)
    return np.where(n < REL_EXACT, n, large).astype(np.int32)


def _toeplitz_tile(bias_h, rows, cols, offset, valid):
    n = rows + cols
    dist = offset - (cols - 1) + np.arange(n)
    table = _bucket_table(max(int(dist.max()) + 1, 1))
    u = jnp.where(valid(dist)[None, :], bias_h[table[np.maximum(dist, 0)]].T, NEG)
    hankel = jnp.tile(u, (1, rows + 1))[:, :rows * (n + 1)].reshape(u.shape[0], rows, n + 1)[:, :, :cols]
    return hankel[:, :, ::-1].astype(F32)


def _swa_bias(bias_a):
    return _toeplitz_tile(bias_a, WINDOW, 2 * WINDOW, WINDOW, lambda d: (d >= 0) & (d < WINDOW))


def _tile_bias(bias_h):
    t, tk = ATT_TILE, ATT_KTILE
    tiles = [_toeplitz_tile(bias_h, t, tk, c * t, lambda d: d >= 0) for c in range(ATT_BIAS_CLASSES)]
    nearest = (ATT_BIAS_CLASSES - 1) * t - (tk - 1)
    assert int(_bucket_table(nearest + 1)[nearest]) == REL_BUCKETS - 1
    return jnp.stack(tiles, axis=1) * LOG2E


def _inproj_kernel(x_ref, g_ref, w_ref, o_ref, xn_ref):
    @pl.when(pl.program_id(1) == 0)
    def _():
        xn_ref[...] = _rms(x_ref[...], g_ref[...]).astype(BF16)

    o_ref[...] = _dot(xn_ref[...], w_ref[...]).astype(o_ref.dtype)


def _inproj(h, g, w, *, tm=512, tn=1280):
    t, d = h.shape
    n = w.shape[1]
    return pl.pallas_call(
        _inproj_kernel,
        grid=(t // tm, n // tn),
        in_specs=[pl.BlockSpec((tm, d), lambda i, j: (i, 0)),
                  pl.BlockSpec((1, d), lambda i, j: (0, 0)),
                  pl.BlockSpec((d, tn), lambda i, j: (0, j))],
        out_specs=pl.BlockSpec((tm, tn), lambda i, j: (i, j)),
        out_shape=jax.ShapeDtypeStruct((t, n), BF16),
        scratch_shapes=[pltpu.VMEM((tm, d), BF16)],
        compiler_params=_params(("parallel", "arbitrary")),
        name="inproj",
    )(h, g, w)


def _outproj_kernel(ya_ref, yb_ref, yc_ref, w_ref, h_ref, o_ref):
    na, nb = ya_ref.shape[1], yb_ref.shape[1]
    y = _dot(ya_ref[...], w_ref[0:na, :])
    y += _dot(yb_ref[...], w_ref[na:na + nb, :])
    y += _dot(yc_ref[...], w_ref[na + nb:, :])
    o_ref[...] = h_ref[...] + y


def _outproj(ya, yb, yc, w, h, *, tm=256):
    t, d = h.shape
    row = lambda width: pl.BlockSpec((tm, width), lambda i: (i, 0))
    return pl.pallas_call(
        _outproj_kernel,
        grid=(t // tm,),
        in_specs=[row(ya.shape[1]), row(yb.shape[1]), row(yc.shape[1]),
                  pl.BlockSpec(w.shape, lambda i: (0, 0)), row(d)],
        out_specs=row(d),
        out_shape=jax.ShapeDtypeStruct((t, d), F32),
        compiler_params=_params(("parallel",)),
        name="outproj",
    )(ya, yb, yc, w, h)


def _swa_kernel(sink_ref, q_ref, kp_ref, kc_ref, vp_ref, vc_ref, bias_ref, o_ref):
    n = pl.program_id(1)
    w = WINDOW
    scale = HEAD_DIM ** -0.5
    col = lax.broadcasted_iota(jnp.int32, (w, 2 * w), 1)
    prev_pen = jnp.where(col < w, jnp.where(n == 0, NEG, 0.0).astype(F32), 0.0)
    for kv in range(SWA_KV_HEADS):
        cols = slice(kv * HEAD_DIM, (kv + 1) * HEAD_DIM)
        kcat = jnp.concatenate([kp_ref[:, cols], kc_ref[:, cols]], axis=0)
        vcat = jnp.concatenate([vp_ref[:, cols], vc_ref[:, cols]], axis=0)
        for g in range(SWA_GROUP):
            hd = kv * SWA_GROUP + g
            hcols = slice(hd * HEAD_DIM, (hd + 1) * HEAD_DIM)
            s = _dot_nt(q_ref[:, hcols], kcat) * scale + bias_ref[hd] + prev_pen
            sink = sink_ref[hd]
            m = jnp.maximum(jnp.max(s, axis=1, keepdims=True), sink)
            p = jnp.exp(s - m)
            denom = jnp.sum(p, axis=1, keepdims=True) + jnp.exp(sink - m)
            o = _dot(p.astype(BF16), vcat) / denom
            o_ref[:, hcols] = o.astype(o_ref.dtype)


def _swa(proj, bias, sinks):
    b, s, _ = proj.shape
    w = WINDOW
    qw, kw = SWA_HEADS * HEAD_DIM, SWA_KV_HEADS * HEAD_DIM
    prev = lambda blk: pl.BlockSpec((None, w, kw), lambda bi, n: (bi, jnp.maximum(n - 1, 0), blk))
    cur = lambda blk: pl.BlockSpec((None, w, kw), lambda bi, n: (bi, n, blk))
    kblk, vblk = (_KA * HEAD_DIM) // kw, (_VA * HEAD_DIM) // kw
    qblk = (_QA * HEAD_DIM) // qw
    return pl.pallas_call(
        _swa_kernel,
        grid=(b, s // w),
        in_specs=[pl.BlockSpec(memory_space=pltpu.SMEM),
                  pl.BlockSpec((None, w, qw), lambda bi, n: (bi, n, qblk)),
                  prev(kblk), cur(kblk), prev(vblk), cur(vblk),
                  pl.BlockSpec(bias.shape, lambda bi, n: (0, 0, 0))],
        out_specs=pl.BlockSpec((None, w, qw), lambda bi, n: (bi, n, 0)),
        out_shape=jax.ShapeDtypeStruct((b, s, qw), BF16),
        compiler_params=_params(("parallel", "parallel")),
        name="swa",
    )(sinks, proj, proj, proj, proj, proj, bias)


def _softmax_step(s, vt, m_ref, l_ref, acc_ref):
    reps = s.shape[1] // HEAD_DIM
    m_prev = m_ref[...]
    m_next = jnp.maximum(m_prev, jnp.max(s, axis=1, keepdims=True))
    alpha = jnp.exp2(m_prev - m_next)
    p = jnp.exp2(s - jnp.concatenate([m_next] * reps, axis=1))
    l_ref[...] = alpha * l_ref[...] + jnp.sum(p, axis=1, keepdims=True)
    acc_ref[...] = alpha * acc_ref[...] + _dot(p.astype(BF16), vt)
    m_ref[...] = m_next


def _softmax_init(m_ref, l_ref, acc_ref):
    m_ref[...] = jnp.full(m_ref.shape, NEG, F32)
    l_ref[...] = jnp.zeros(l_ref.shape, F32)
    acc_ref[...] = jnp.zeros(acc_ref.shape, F32)


def _moba_kernel(q_ref, k_ref, v_ref, bias_ref, o_ref, km_ref, pen_ref, m_ref, l_ref, acc_ref):
    j = pl.program_id(1)
    t, tk = ATT_TILE, ATT_KTILE
    blocks_per_tile = tk // t
    nb = k_ref.shape[0] // t
    heads = q_ref.shape[1] // HEAD_DIM
    scale = HEAD_DIM ** -0.5
    hcols = [slice(h * HEAD_DIM, (h + 1) * HEAD_DIM) for h in range(heads)]

    @pl.when(j == 0)
    def _():
        km_ref[...] = jnp.zeros_like(km_ref)
        for n in range(nb):
            kmean = jnp.mean(k_ref[n * t:(n + 1) * t, :].astype(F32), axis=0, keepdims=True)
            for h in range(heads):
                km_ref[h, n:n + 1, :] = kmean[:, hcols[h]]

    lane = lax.broadcasted_iota(jnp.int32, (t, ROUTE_LANES), 1)
    lane_f = lane.astype(F32)
    past = lane < j
    for h in range(heads):
        km_hi, km_lo = _split_bf16(km_ref[h])
        q = q_ref[:, hcols[h]]
        gate = _dot_nt(q, km_hi) + _dot_nt(q, km_lo)
        gate = jnp.where(past, gate, -jnp.inf)
        pen = jnp.full(gate.shape, NEG, F32)
        for _ in range(MOBA_TOPK):
            best = jnp.max(gate, axis=1, keepdims=True)
            first = jnp.min(jnp.where(gate == best, lane_f, float(ROUTE_LANES)), axis=1, keepdims=True)
            pick = (lane_f == first) & past
            pen = jnp.where(pick, 0.0, pen)
            gate = jnp.where(pick, -jnp.inf, gate)
        pen_ref[h] = jnp.where(lane == j, 0.0, pen)

    _softmax_init(m_ref, l_ref, acc_ref)

    def body(n, c):
        start = pl.multiple_of(n * tk, tk)
        cls = jnp.minimum(j - blocks_per_tile * n, ATT_BIAS_CLASSES - 1)
        for h in range(heads):
            kt = k_ref[pl.ds(start, tk), hcols[h]]
            vt = v_ref[pl.ds(start, tk), hcols[h]]
            pen = pen_ref[h]
            pens = [jnp.broadcast_to(jnp.sum(jnp.where(lane == blocks_per_tile * n + i, pen, 0.0), axis=1,
                                             keepdims=True), (t, t)) for i in range(blocks_per_tile)]
            s = _dot_nt(q_ref[:, hcols[h]], kt) * (scale * LOG2E) + bias_ref[h, cls]
            s = s + jnp.concatenate(pens, axis=1)
            _softmax_step(s, vt, m_ref.at[h], l_ref.at[h], acc_ref.at[h])
        return c

    lax.fori_loop(0, j // blocks_per_tile + 1, body, 0)
    for h in range(heads):
        o_ref[:, hcols[h]] = (acc_ref[h] / l_ref[h]).astype(o_ref.dtype)


def _moba(proj, bias):
    b, s, _ = proj.shape
    t = ATT_TILE
    heads = MOBA_HEADS
    width = heads * HEAD_DIM
    blk = lambda off: (off * HEAD_DIM) // width
    return pl.pallas_call(
        _moba_kernel,
        grid=(b, s // t),
        in_specs=[pl.BlockSpec((None, t, width), lambda bi, j: (bi, j, blk(_QB))),
                  pl.BlockSpec((None, s, width), lambda bi, j: (bi, 0, blk(_KB))),
                  pl.BlockSpec((None, s, width), lambda bi, j: (bi, 0, blk(_VB))),
                  pl.BlockSpec(bias.shape, lambda bi, j: (0, 0, 0, 0))],
        out_specs=pl.BlockSpec((None, t, width), lambda bi, j: (bi, j, 0)),
        out_shape=jax.ShapeDtypeStruct((b, s, width), BF16),
        scratch_shapes=[pltpu.VMEM((heads, ROUTE_LANES, HEAD_DIM), F32), pltpu.VMEM((heads, t, ROUTE_LANES), F32)]
        + [pltpu.VMEM((heads, t, HEAD_DIM), F32)] * 3,
        compiler_params=_params(("arbitrary", "arbitrary")),
        name="moba",
    )(proj, proj, proj, bias)


def _diff_kernel(q_ref, k_ref, v_ref, bias_ref, lam_ref, subg_ref, o_ref, m_ref, l_ref, acc_ref, *, lambda_init):
    qi = pl.program_id(2)
    t, tk = ATT_TILE, ATT_KTILE
    heads = q_ref.shape[1] // HEAD_DIM
    hcols = [slice(h * HEAD_DIM, (h + 1) * HEAD_DIM) for h in range(heads)]
    lane = lax.broadcasted_iota(jnp.int32, (t, HEAD_DIM), 1)
    qs = []
    for h in range(heads):
        q = (q_ref[:, hcols[h]].astype(F32) * (DIFF_QK_DIM ** -0.5)).astype(BF16)
        zero = jnp.zeros_like(q)
        qs.append(jnp.where(lane < DIFF_QK_DIM, q, zero))
        qs.append(jnp.where(lane >= DIFF_QK_DIM, q, zero))

    _softmax_init(m_ref, l_ref, acc_ref)

    def body(n, c):
        start = pl.multiple_of(n * tk, tk)
        cls = jnp.minimum(qi - (tk // t) * n, ATT_BIAS_CLASSES - 1)
        for h in range(heads):
            kt = k_ref[pl.ds(start, tk), hcols[h]]
            vt = v_ref[pl.ds(start, tk), hcols[h]]
            bias = bias_ref[h, cls]
            for half in range(2):
                st = 2 * h + half
                _softmax_step(_dot_nt(qs[st], kt) * LOG2E + bias, vt, m_ref.at[st], l_ref.at[st], acc_ref.at[st])
        return c

    lax.fori_loop(0, qi // (tk // t) + 1, body, 0)
    lv = lam_ref[...]
    lam = (jnp.exp(jnp.sum(lv[0:1] * lv[1:2], axis=1, keepdims=True))
           - jnp.exp(jnp.sum(lv[2:3] * lv[3:4], axis=1, keepdims=True)) + lambda_init)
    for h in range(heads):
        o = acc_ref[2 * h] / l_ref[2 * h] - lam * (acc_ref[2 * h + 1] / l_ref[2 * h + 1])
        o_ref[:, hcols[h]] = (_rms(o, subg_ref[...]) * (1.0 - lambda_init)).astype(o_ref.dtype)


def _diff(proj, bias, lam_rows, subg, lambda_init):
    b, s, _ = proj.shape
    t = ATT_TILE
    heads = DIFF_HEADS_PER_STEP
    width = heads * HEAD_DIM
    blk = lambda off: (off * HEAD_DIM) // width
    return pl.pallas_call(
        functools.partial(_diff_kernel, lambda_init=lambda_init),
        grid=(b, DIFF_HEADS // heads, s // t),
        in_specs=[pl.BlockSpec((None, t, width), lambda bi, h, j: (bi, j, blk(_QC) + h)),
                  pl.BlockSpec((None, s, width), lambda bi, h, j: (bi, 0, blk(_KC) + h)),
                  pl.BlockSpec((None, s, width), lambda bi, h, j: (bi, 0, blk(_VC) + h)),
                  pl.BlockSpec((heads,) + bias.shape[1:], lambda bi, h, j: (h, 0, 0, 0)),
                  pl.BlockSpec(lam_rows.shape, lambda bi, h, j: (0, 0)),
                  pl.BlockSpec(subg.shape, lambda bi, h, j: (0, 0))],
        out_specs=pl.BlockSpec((None, t, width), lambda bi, h, j: (bi, j, h)),
        out_shape=jax.ShapeDtypeStruct((b, s, DIFF_HEADS * HEAD_DIM), BF16),
        scratch_shapes=[pltpu.VMEM((2 * heads, t, HEAD_DIM), F32)] * 3,
        compiler_params=_params(("parallel", "parallel", "parallel")),
        name="diff",
    )(proj, proj, proj, bias, lam_rows, subg)


def _router_kernel(h_ref, g_ref, whi_ref, wlo_ref, xn_ref, route_ref, routet_ref, cnt_ref, carry_ref):
    i = pl.program_id(0)
    tm = h_ref.shape[0]

    @pl.when(i == 0)
    def _():
        carry_ref[...] = jnp.zeros_like(carry_ref)

    xn = _rms(h_ref[...], g_ref[...])
    xn_ref[...] = xn
    x_hi, x_lo = _split_bf16(xn)
    lg = _dot(x_hi, whi_ref[...]) + (_dot(x_hi, wlo_ref[...]) + _dot(x_lo, whi_ref[...]))
    lane = lax.broadcasted_iota(jnp.int32, lg.shape, 1).astype(F32)
    big = float(ROUTE_LANES)

    def first_max(vals):
        best = jnp.max(vals, axis=1, keepdims=True)
        return best, jnp.min(jnp.where(vals == best, lane, big), axis=1, keepdims=True)

    gl = jnp.where(lane < N_GROUPS, lg, -jnp.inf)
    gmax, grp = first_max(gl)
    g_w = 1.0 / jnp.sum(jnp.exp(gl - gmax), axis=1, keepdims=True)
    lo = EXPERT_LANE0 + grp * EXPERTS_PER_GROUP
    el = jnp.where((lane >= lo) & (lane < lo + EXPERTS_PER_GROUP), lg, -jnp.inf)
    m1, i1 = first_max(el)
    m2, i2 = first_max(jnp.where(lane == i1, -jnp.inf, el))
    t2 = jnp.exp(m2 - m1)
    w0 = g_w / (1.0 + t2)
    w1 = w0 * t2
    onehot = jnp.where((lane == i1) | (lane == i2), 1.0, 0.0)
    r = lax.broadcasted_iota(jnp.int32, (tm, tm), 0)
    c = lax.broadcasted_iota(jnp.int32, (tm, tm), 1)
    before = jnp.where(c < r, 1.0, 0.0).astype(BF16)
    counts = _dot(before, onehot.astype(BF16)) + carry_ref[0:1, :]
    rank0 = jnp.sum(jnp.where(lane == i1, counts, 0.0), axis=1, keepdims=True)
    rank1 = jnp.sum(jnp.where(lane == i2, counts, 0.0), axis=1, keepdims=True)
    total = carry_ref[0:1, :] + jnp.sum(onehot, axis=0, keepdims=True)
    carry_ref[0:1, :] = total
    cnt_ref[...] = jnp.broadcast_to(total, cnt_ref.shape)
    rec = jnp.zeros(lg.shape, F32)
    for k, val in enumerate((i1 - EXPERT_LANE0, i2 - EXPERT_LANE0, w0, w1, rank0, rank1)):
        rec = jnp.where(lane == k, val, rec)
    route_ref[...] = rec
    routet_ref[...] = rec.T[0:routet_ref.shape[0], :]


def _router(h, g, w_hi, w_lo, *, tm=256):
    t, d = h.shape
    full = lambda a: pl.BlockSpec(a.shape, lambda i: (0, 0))
    return pl.pallas_call(
        _router_kernel,
        grid=(t // tm,),
        in_specs=[pl.BlockSpec((tm, d), lambda i: (i, 0)), full(g), full(w_hi), full(w_lo)],
        out_specs=[pl.BlockSpec((tm, d), lambda i: (i, 0)),
                   pl.BlockSpec((tm, ROUTE_LANES), lambda i: (i, 0)),
                   pl.BlockSpec((8, tm), lambda i: (0, i)),
                   pl.BlockSpec((8, ROUTE_LANES), lambda i: (0, 0))],
        out_shape=[jax.ShapeDtypeStruct((t, d), F32),
                   jax.ShapeDtypeStruct((t, ROUTE_LANES), F32),
                   jax.ShapeDtypeStruct((8, t), F32),
                   jax.ShapeDtypeStruct((8, ROUTE_LANES), F32)],
        scratch_shapes=[pltpu.VMEM((8, ROUTE_LANES), F32)],
        compiler_params=_params(("arbitrary",)),
        name="router",
    )(h, g, w_hi, w_lo)


def _row_copy(src_hbm, dst_vmem, sem, src_row, dst_row):
    return pltpu.make_async_copy(src_hbm.at[pl.ds(src_row, 1)], dst_vmem.at[pl.ds(dst_row, 1)], sem)


def _start_rows(src_hbm, dst_vmem, sem, idx_ref, base, rows):
    for r in range(rows):
        _row_copy(src_hbm, dst_vmem, sem, idx_ref[base + r], r).start()


def _wait_rows(src_hbm, dst_vmem, sem, rows):
    def wait(r, c):
        _row_copy(src_hbm, dst_vmem, sem, 0, r).wait()
        return c

    lax.fori_loop(0, rows, wait, 0, unroll=8)


def _ffn_kernel(blk_e_ref, tok_ref, nused_ref, xn_hbm, w1_ref, w3_ref, w2_ref, o_ref, xbuf, xb_ref, sem):
    i = pl.program_id(0)
    nused = nused_ref[0]
    rows = MOE_BLOCK
    slot = lax.rem(i, 2)

    @pl.when(i == 0)
    def _():
        _start_rows(xn_hbm, xbuf.at[0], sem.at[0], tok_ref, 0, rows)

    @pl.when(i < nused)
    def _():
        _wait_rows(xn_hbm, xbuf.at[slot], sem.at[slot], rows)
        xb_ref[...] = xbuf[slot].astype(BF16)

    def compute():
        xb = xb_ref[...]
        a = _dot(xb, w1_ref[...])
        hdn = (a * jax.nn.sigmoid(a) * _dot(xb, w3_ref[...])).astype(BF16)
        o_ref[...] = _dot(hdn, w2_ref[...])

    @pl.when(i + 1 < nused)
    def _():
        _start_rows(xn_hbm, xbuf.at[1 - slot], sem.at[1 - slot], tok_ref, (i + 1) * rows, rows)
        compute()

    @pl.when(i + 1 == nused)
    def _():
        compute()

    @pl.when(i >= nused)
    def _():
        o_ref[...] = jnp.zeros_like(o_ref)


def _ffn(blk_e, buf_tok, nused, xn, w1, w3, w2):
    t, d = xn.shape
    p_rows = buf_tok.shape[0]
    de = w1.shape[2]
    return pl.pallas_call(
        _ffn_kernel,
        grid_spec=pltpu.PrefetchScalarGridSpec(
            num_scalar_prefetch=3,
            grid=(p_rows // MOE_BLOCK,),
            in_specs=[pl.BlockSpec(memory_space=pl.ANY),
                      pl.BlockSpec((None, d, de), lambda i, be, tk, nu: (be[i], 0, 0)),
                      pl.BlockSpec((None, d, de), lambda i, be, tk, nu: (be[i], 0, 0)),
                      pl.BlockSpec((None, de, d), lambda i, be, tk, nu: (be[i], 0, 0))],
            out_specs=pl.BlockSpec((MOE_BLOCK, d), lambda i, be, tk, nu: (i, 0)),
            scratch_shapes=[pltpu.VMEM((2, MOE_BLOCK, d), F32), pltpu.VMEM((MOE_BLOCK, d), BF16),
                            pltpu.SemaphoreType.DMA((2,))]),
        out_shape=jax.ShapeDtypeStruct((p_rows, d), F32),
        compiler_params=_params(("arbitrary",)),
        name="ffn",
    )(blk_e, buf_tok, nused, xn, w1, w3, w2)


def _combine_ple_kernel(d0_ref, d1_ref, h_ref, route_ref, ys_hbm, g_ref, wg_ref, p_ref, wp_ref, gf_ref, o_ref,
                        buf, h2_ref, sem, *, final):
    i = pl.program_id(0)
    steps = pl.num_programs(0)
    rows = h_ref.shape[0]
    slot = lax.rem(i, 2)

    def start_tile(tile, slot_):
        _start_rows(ys_hbm, buf.at[slot_, 0], sem.at[slot_], d0_ref, tile * rows, rows)
        _start_rows(ys_hbm, buf.at[slot_, 1], sem.at[slot_], d1_ref, tile * rows, rows)

    @pl.when(i == 0)
    def _():
        start_tile(0, 0)

    _wait_rows(ys_hbm, buf.at[slot, 0], sem.at[slot], rows)
    _wait_rows(ys_hbm, buf.at[slot, 1], sem.at[slot], rows)
    rec = route_ref[...]
    h2_ref[...] = h_ref[...] + (rec[:, 2:3] * buf[slot, 0] + rec[:, 3:4] * buf[slot, 1])

    def compute():
        h2 = h2_ref[...]
        u = _rms(h2, g_ref[...]).astype(BF16)
        gate = jax.nn.sigmoid(_dot(u, wg_ref[...]))
        o = h2 + gate * _dot(p_ref[...].astype(BF16), wp_ref[...])
        if final:
            o = _rms(o, gf_ref[...])
        o_ref[...] = o

    @pl.when(i + 1 < steps)
    def _():
        start_tile(i + 1, 1 - slot)
        compute()

    @pl.when(i + 1 == steps)
    def _():
        compute()


def _combine_ple(dest0, dest1, h, route, ys, g, wg, p, wp, g_final, *, final, tm=256):
    t, d = h.shape
    row = lambda width: pl.BlockSpec((tm, width), lambda i, a, b: (i, 0))
    full = lambda arr: pl.BlockSpec(arr.shape, lambda i, a, b: (0, 0))
    return pl.pallas_call(
        functools.partial(_combine_ple_kernel, final=final),
        grid_spec=pltpu.PrefetchScalarGridSpec(
            num_scalar_prefetch=2,
            grid=(t // tm,),
            in_specs=[row(d), row(ROUTE_LANES), pl.BlockSpec(memory_space=pl.ANY),
                      full(g), full(wg), row(p.shape[1]), full(wp), full(g_final)],
            out_specs=row(d),
            scratch_shapes=[pltpu.VMEM((2, 2, tm, d), F32), pltpu.VMEM((tm, d), F32),
                            pltpu.SemaphoreType.DMA((2,))]),
        out_shape=jax.ShapeDtypeStruct((t, d), F32),
        compiler_params=_params(("arbitrary",)),
        name="combine_ple",
    )(dest0, dest1, h, route, ys, g, wg, p, wp, g_final)


def _moe_experts(h, g, w_hi, w_lo, w1, w3, w2):
    t = h.shape[0]
    xn, route, route_t, cnt = _router(h, g, w_hi, w_lo)
    counts = cnt[0, EXPERT_LANE0:EXPERT_LANE0 + N_EXPERTS].astype(jnp.int32)
    pcounts = ((counts + MOE_BLOCK - 1) // MOE_BLOCK) * MOE_BLOCK
    pends = jnp.cumsum(pcounts)
    pstarts = pends - pcounts
    experts = jnp.arange(N_EXPERTS, dtype=jnp.int32)[:, None]
    seg_start = lambda e: jnp.sum(jnp.where(e[None, :] == experts, pstarts[:, None], 0), axis=0)
    e0, e1, rank0, rank1 = (route_t[k].astype(jnp.int32) for k in (0, 1, 4, 5))
    dest0 = seg_start(e0) + rank0
    dest1 = seg_start(e1) + rank1
    p_rows = 2 * t + N_EXPERTS * MOE_BLOCK
    tok = jnp.arange(t, dtype=jnp.int32)
    buf_tok = jnp.zeros((p_rows,), jnp.int32).at[jnp.concatenate([dest0, dest1])].set(jnp.concatenate([tok, tok]))
    nblk = p_rows // MOE_BLOCK
    blk_start = jnp.arange(nblk, dtype=jnp.int32)[:, None] * MOE_BLOCK
    blk_e = jnp.minimum(jnp.sum((pends[None, :] <= blk_start).astype(jnp.int32), axis=1), N_EXPERTS - 1)
    nused = (pends[-1:] // MOE_BLOCK).astype(jnp.int32)
    ys = _ffn(blk_e, buf_tok, nused, xn, w1, w3, w2)
    return dest0, dest1, route, ys


def kernel(x, p, rel_bias, g_final, g_mix, w_in, swa_sinks, lam_q1, lam_k1, lam_q2, lam_k2, diff_subln, w_out,
           g_ffn, w_group, w_expert, w1, w3, w2, g_ple, w_ple_gate, w_ple_proj):
    b, s, d = x.shape
    depth = w_in.shape[0]
    t = b * s
    bias_a = _swa_bias(rel_bias[:, :SWA_HEADS])
    bias_b = _tile_bias(rel_bias[:, SWA_HEADS:SWA_HEADS + MOBA_HEADS])
    bias_c = _tile_bias(rel_bias[:, SWA_HEADS + MOBA_HEADS:])
    w_in = jnp.concatenate([w_in[..., lo:hi] for lo, hi in (_REF_GROUP_COLS[g] for g in "bca")], axis=-1)
    w_in, w_out, w1, w3, w2, w_ple_gate, w_ple_proj = (
        a.astype(BF16) for a in (w_in, w_out, w1, w3, w2, w_ple_gate, w_ple_proj))
    w_route = jnp.concatenate(
        [w_group, w_expert, jnp.zeros((depth, d, ROUTE_LANES - N_GROUPS - N_EXPERTS), F32)], axis=-1)
    w_route_hi = w_route.astype(BF16)
    w_route_lo = (w_route - w_route_hi.astype(F32)).astype(BF16)
    lam_rows = jnp.stack([lam_q1, lam_k1, lam_q2, lam_k2], axis=1).astype(F32)
    lam_rows = jnp.pad(lam_rows, ((0, 0), (0, 4), (0, HEAD_DIM - DIFF_QK_DIM)))
    row = lambda a: a.reshape(1, -1).astype(F32)

    h = x.reshape(t, d)
    for i in range(depth):
        proj = _inproj(h, row(g_mix[i]), w_in[i]).reshape(b, s, IN_WIDTH)
        ya = _swa(proj, bias_a, swa_sinks[i].astype(F32))
        yb = _moba(proj, bias_b)
        lambda_init = 0.8 - 0.6 * math.exp(-0.3 * i)
        yc = _diff(proj, bias_c, lam_rows[i], row(diff_subln[i]), lambda_init)
        h = _outproj(ya.reshape(t, -1), yb.reshape(t, -1), yc.reshape(t, -1), w_out[i], h)
        dest0, dest1, route, ys = _moe_experts(h, row(g_ffn[i]), w_route_hi[i], w_route_lo[i], w1[i], w3[i], w2[i])
        h = _combine_ple(dest0, dest1, h, route, ys, row(g_ple[i]), w_ple_gate[i], p[i].reshape(t, PLE_DIM),
                         w_ple_proj[i], row(g_final), final=(i == depth - 1))
    return h.reshape(b, s, d)
```
